```python
import math
import jax, jax.numpy as jnp
from jax import lax
import numpy as np

D_MODEL = 1024
BATCH = 1
SEQ = 16384
DEPTH = 2
DEC_BATCH = 32
DEC_SEQ = 1
PAST_LEN = 16384
PAGE_SIZE = 128

D_MIX = D_MODEL
A_HEADS = 4
A_QK = 32
A_HD = 2 * A_QK
M_HEADS = 8
M_HD = 64
M_INNER = M_HEADS * M_HD
M_GROUPS = 2
M_STATE = 128
CONV_W = 4
M_CONV_DIM = M_INNER + 2 * M_GROUPS * M_STATE
M_CHUNK = 128
C_HEADS = 4
C_HD = 64
CMP_BLOCK = 64
SEL_BLOCK = 64
N_SEL = 16
WINDOW = 512
Q_BLOCK = 128
D_FF = 2816
EPS = 1e-6
NEG = -1e30
BIG = 1e30

IN_SPLITS = (A_HEADS * A_HD, A_HEADS * A_HD, A_HEADS * A_HD,
             M_INNER, M_CONV_DIM, M_HEADS,
             C_HEADS * C_HD, 6 * C_HD, 3 * C_HEADS)
D_IN_PROJ = sum(IN_SPLITS)

kernel_name = 'hymba_style_diff_ssd_nsa_macaron_decode'


def rmsnorm(x, g, eps=EPS):
    xf = x.astype(jnp.float32)
    y = xf * lax.rsqrt(jnp.mean(xf * xf, axis=-1, keepdims=True) + eps)
    return (y * g.astype(jnp.float32)).astype(x.dtype)


def swiglu_ffn(x, g, wg, wu, wd):
    h = rmsnorm(x, g)
    return (jax.nn.silu(h @ wg) * (h @ wu)) @ wd


def masked_softmax(s, mask):
    s = jnp.where(mask, s.astype(jnp.float32), NEG)
    m = jnp.max(s, axis=-1, keepdims=True)
    e = jnp.where(mask, jnp.exp(s - m), 0.0)
    den = jnp.sum(e, axis=-1, keepdims=True)
    return e / jnp.where(den > 0, den, 1.0)


def unblock(o):
    nblk, b, qb = o.shape[0], o.shape[1], o.shape[2]
    return jnp.moveaxis(o, 0, 1).reshape((b, nblk * qb) + tuple(o.shape[3:]))


def in_project(h, lp):
    B, L, _ = h.shape
    u = h @ lp['w_in']
    offs = [0]
    for w in IN_SPLITS:
        offs.append(offs[-1] + w)
    aq, ak, av, z, xbc, dt_raw, cq, ckv, cg = [u[..., offs[i]:offs[i + 1]] for i in range(len(IN_SPLITS))]
    q = rmsnorm(aq.reshape(B, L, A_HEADS, 2, A_QK), lp['diff_q_norm'])
    k = rmsnorm(ak.reshape(B, L, A_HEADS, 2, A_QK), lp['diff_k_norm'])
    v = av.reshape(B, L, A_HEADS, A_HD)
    diff_rows = jnp.stack([k.reshape(B, L, A_HEADS, A_HD), v], axis=2)
    cq = rmsnorm(cq.reshape(B, L, C_HEADS, C_HD), lp['nsa_q_norm'])
    ckv = ckv.reshape(B, L, 6, C_HD)
    k_sel = rmsnorm(ckv[:, :, 2], lp['nsa_k_norm'][1])
    k_win = rmsnorm(ckv[:, :, 4], lp['nsa_k_norm'][2])
    nsa_rows = jnp.stack([ckv[:, :, 0], ckv[:, :, 1], k_sel, ckv[:, :, 3]], axis=2)
    win_rows = jnp.stack([k_win, ckv[:, :, 5]], axis=2)
    gates = jax.nn.sigmoid(cg.reshape(B, L, C_HEADS, 3))
    return q, diff_rows, z, xbc, dt_raw, cq, nsa_rows, win_rows, gates


def diff_attn_core(q, k, v, qpos, kpos, lam):
    s = jnp.einsum('bqhcd,bkhcd->bhcqk', q, k).astype(jnp.float32) * (A_QK ** -0.5)
    mask = kpos[None, :] <= qpos[:, None]
    p = jax.nn.softmax(jnp.where(mask, s, NEG), axis=-1)
    a = p[:, :, 0] - lam * p[:, :, 1]
    return jnp.einsum('bhqk,bkhd->bqhd', a.astype(v.dtype), v)


def ssd_scan(x, dt, a, bm, cm, h0):
    B, L, H, P = x.shape
    q = min(M_CHUNK, L)
    nc = -(-L // q)
    pad = nc * q - L
    rep = H // M_GROUPS
    xf = x.astype(jnp.float32)
    dt = dt.astype(jnp.float32)
    bh = jnp.repeat(bm.astype(jnp.float32), rep, axis=2)
    ch = jnp.repeat(cm.astype(jnp.float32), rep, axis=2)
    if pad:
        padw = lambda t: jnp.pad(t, [(0, 0), (0, pad)] + [(0, 0)] * (t.ndim - 2))
        xf, dt, bh, ch = padw(xf), padw(dt), padw(bh), padw(ch)
    xf = xf.reshape(B, nc, q, H, P)
    dt = dt.reshape(B, nc, q, H)
    bh = bh.reshape(B, nc, q, H, M_STATE)
    ch = ch.reshape(B, nc, q, H, M_STATE)
    acum = jnp.cumsum(dt * a, axis=2)
    xdt = xf * dt[..., None]
    causal = jnp.tril(jnp.ones((q, q), dtype=jnp.bool_))[None, None, :, :, None]
    seg = acum[:, :, :, None, :] - acum[:, :, None, :, :]
    lmat = jnp.where(causal, jnp.exp(jnp.where(causal, seg, 0.0)), 0.0)
    cb = jnp.einsum('bclhn,bcshn->bclsh', ch, bh)
    y_diag = jnp.einsum('bclsh,bcshp->bclhp', cb * lmat, xdt)
    decay = jnp.exp(acum[:, :, -1:, :] - acum)
    states = jnp.einsum('bclhn,bclh,bclhp->bchpn', bh, decay, xdt)
    chunk_decay = jnp.exp(acum[:, :, -1, :])

    def step(hc, inp):
        st, dec = inp
        return hc * dec[:, :, None, None] + st, hc

    h_fin, h_prev = lax.scan(step, h0.astype(jnp.float32),
                             (jnp.moveaxis(states, 1, 0), jnp.moveaxis(chunk_decay, 1, 0)))
    h_prev = jnp.moveaxis(h_prev, 0, 1)
    y_off = jnp.einsum('bclhn,bchpn,bclh->bclhp', ch, h_prev, jnp.exp(acum))
    y = (y_diag + y_off).reshape(B, nc * q, H, P)[:, :L]
    return y, h_fin


def causal_depthwise_conv(full, w, b, L):
    acc = full[:, 0:L] * w[0]
    for j in range(1, CONV_W):
        acc = acc + full[:, j:j + L] * w[j]
    return acc + b


def mamba_mix(z, xbc, dt_raw, conv_prev, h0, lp):
    B, L, _ = xbc.shape
    full = jnp.concatenate([conv_prev.astype(xbc.dtype), xbc], axis=1)
    xbc_a = jax.nn.silu(causal_depthwise_conv(full, lp['conv_w'].astype(full.dtype),
                                              lp['conv_b'].astype(full.dtype), L))
    x = xbc_a[..., :M_INNER].reshape(B, L, M_HEADS, M_HD)
    bm = xbc_a[..., M_INNER:M_INNER + M_GROUPS * M_STATE].reshape(B, L, M_GROUPS, M_STATE)
    cm = xbc_a[..., M_INNER + M_GROUPS * M_STATE:].reshape(B, L, M_GROUPS, M_STATE)
    dt = jax.nn.softplus(dt_raw.astype(jnp.float32) + lp['dt_bias'].astype(jnp.float32))
    a = -jnp.exp(lp['a_log'].astype(jnp.float32))
    y, h_fin = ssd_scan(x, dt, a, bm, cm, h0)
    y = y.astype(x.dtype) + x * lp['d_skip'][:, None].astype(x.dtype)
    g = (y.reshape(B, L, M_INNER) * jax.nn.silu(z)).reshape(B, L, M_GROUPS, M_INNER // M_GROUPS)
    y = rmsnorm(g, lp['ssm_norm'].reshape(M_GROUPS, M_INNER // M_GROUPS)).reshape(B, L, M_INNER)
    return y, h_fin, full[:, -(CONV_W - 1):]


def nsa_compress(cmp_rows, lp):
    B, T = cmp_rows.shape[0], cmp_rows.shape[1]
    nb = T // CMP_BLOCK
    blk = cmp_rows[:, :nb * CMP_BLOCK].reshape(B, nb, CMP_BLOCK, 2, C_HD) + lp['nsa_cmp_pe']
    out = jnp.einsum('bnlcd,clde->bnce', blk, lp['nsa_cmp_w'])
    return rmsnorm(out[:, :, 0], lp['nsa_k_norm'][0]), out[:, :, 1]


def nsa_core(q, pos, gates, kc, vc, gather_blocks, n_blocks, kw, vw, wpos):
    B, Lq = q.shape[0], q.shape[1]
    scale = C_HD ** -0.5
    nb = kc.shape[1]
    cmask = (jnp.arange(1, nb + 1) * CMP_BLOCK - 1)[None, :] <= pos[:, None]
    p_cmp = masked_softmax(jnp.einsum('bqhd,bnd->bhqn', q, kc) * scale, cmask)
    o_cmp = jnp.einsum('bhqn,bnd->bqhd', p_cmp.astype(vc.dtype), vc)
    imp = jnp.pad(jnp.sum(p_cmp, axis=1), ((0, 0), (0, 0), (0, n_blocks - nb)))
    blk = jnp.arange(n_blocks)[None, :]
    cur = (pos // SEL_BLOCK)[:, None]
    score = jnp.where(blk == cur, BIG, jnp.where(blk < cur, imp, NEG))
    top_v, idx = lax.top_k(score, min(N_SEL, n_blocks))
    n = idx.shape[-1]
    kv = gather_blocks(idx)
    tok = idx[..., None] * SEL_BLOCK + jnp.arange(SEL_BLOCK)
    smask = (top_v > 0.5 * NEG)[..., None] & (tok <= pos[None, :, None, None])
    smask = smask.reshape(B, 1, Lq, n * SEL_BLOCK)
    kg = kv[..., 0, :].reshape(B, Lq, n * SEL_BLOCK, C_HD)
    vg = kv[..., 1, :].reshape(B, Lq, n * SEL_BLOCK, C_HD)
    p_sel = masked_softmax(jnp.einsum('bqhd,bqmd->bhqm', q, kg) * scale, smask)
    o_sel = jnp.einsum('bhqm,bqmd->bqhd', p_sel.astype(vg.dtype), vg)
    wmask = ((wpos[None, :] <= pos[:, None]) & (wpos[None, :] >= pos[:, None] - WINDOW)
             & (wpos[None, :] >= 0))
    p_win = masked_softmax(jnp.einsum('bqhd,bkd->bhqk', q, kw) * scale, wmask)
    o_win = jnp.einsum('bhqk,bkd->bqhd', p_win.astype(vw.dtype), vw)
    return gates[..., 0:1] * o_cmp + gates[..., 1:2] * o_sel + gates[..., 2:3] * o_win


def mix_out(o_diff, o_ssm, o_nsa, w_out):
    B, L = o_ssm.shape[0], o_ssm.shape[1]
    cat = jnp.concatenate([o_diff.reshape(B, L, A_HEADS * A_HD).astype(o_ssm.dtype), o_ssm,
                           o_nsa.reshape(B, L, C_HEADS * C_HD).astype(o_ssm.dtype)], axis=-1)
    return cat @ w_out


def mix_prompt(h, lp, lam, lam_init):
    B, S, _ = h.shape
    qb = min(Q_BLOCK, S)
    starts = jnp.arange(S // qb, dtype=jnp.int32) * qb
    q, diff_rows, z, xbc, dt_raw, cq, nsa_rows, win_rows, gates = in_project(h, lp)
    k_all = diff_rows[:, :, 0].reshape(B, S, A_HEADS, 2, A_QK)
    v_all = diff_rows[:, :, 1]
    kpos = jnp.arange(S, dtype=jnp.int32)

    def diff_block(s0):
        return diff_attn_core(lax.dynamic_slice_in_dim(q, s0, qb, 1), k_all, v_all,
                              s0 + jnp.arange(qb, dtype=jnp.int32), kpos, lam)

    o_diff = rmsnorm(unblock(lax.map(diff_block, starts)), lp['diff_subln']) * (1.0 - lam_init)
    o_ssm, ssm_f, conv_f = mamba_mix(z, xbc, dt_raw,
                                     jnp.zeros((B, CONV_W - 1, M_CONV_DIM), h.dtype),
                                     jnp.zeros((B, M_HEADS, M_HD, M_STATE), jnp.float32), lp)
    kc, vc = nsa_compress(nsa_rows[:, :, :2], lp)
    n_blocks = -(-S // SEL_BLOCK)
    sel_blocks = jnp.pad(nsa_rows[:, :, 2:], ((0, 0), (0, n_blocks * SEL_BLOCK - S), (0, 0), (0, 0)))
    sel_blocks = sel_blocks.reshape(B, n_blocks, SEL_BLOCK, 2, C_HD)

    def gather_blocks(idx):
        return jax.vmap(lambda blks, i: blks[i])(sel_blocks, idx)

    w_pad = jnp.pad(win_rows, ((0, 0), (WINDOW, 0), (0, 0), (0, 0)))

    def nsa_block(s0):
        band = lax.dynamic_slice_in_dim(w_pad, s0, WINDOW + qb, 1)
        return nsa_core(lax.dynamic_slice_in_dim(cq, s0, qb, 1), s0 + jnp.arange(qb, dtype=jnp.int32),
                        lax.dynamic_slice_in_dim(gates, s0, qb, 1), kc, vc, gather_blocks, n_blocks,
                        band[:, :, 0], band[:, :, 1],
                        s0 - WINDOW + jnp.arange(WINDOW + qb, dtype=jnp.int32))

    o_nsa = unblock(lax.map(nsa_block, starts))
    out = mix_out(o_diff, o_ssm, o_nsa, lp['w_out'])
    return out, diff_rows, nsa_rows, win_rows[:, S - min(WINDOW, S):], ssm_f, conv_f


def mix_sample(h, pool_d, pool_n, win_state, ssm_state, conv_state, page_table, lp, lam, lam_init):
    B, L, _ = h.shape
    page = pool_d.shape[1]
    n_pages = page_table.shape[1]
    past = n_pages * page
    T = past + L
    qpos = past + jnp.arange(L, dtype=jnp.int32)
    q, diff_rows, z, xbc, dt_raw, cq, nsa_rows, win_rows, gates = in_project(h, lp)
    past_d = jnp.take(pool_d, page_table, axis=0)
    k_past = past_d[:, :, :, 0].reshape(B, past, A_HEADS, 2, A_QK)
    v_past = past_d[:, :, :, 1].reshape(B, past, A_HEADS, A_HD)
    k_all = jnp.concatenate([k_past.astype(q.dtype), diff_rows[:, :, 0].reshape(B, L, A_HEADS, 2, A_QK)], axis=1)
    v_all = jnp.concatenate([v_past.astype(q.dtype), diff_rows[:, :, 1]], axis=1)
    o_diff = diff_attn_core(q, k_all, v_all, qpos, jnp.arange(T, dtype=jnp.int32), lam)
    o_diff = rmsnorm(o_diff, lp['diff_subln']) * (1.0 - lam_init)
    o_ssm, ssm_f, conv_f = mamba_mix(z, xbc, dt_raw, conv_state, ssm_state, lp)
    past_n = jnp.take(pool_n, page_table, axis=0)
    cmp_past = past_n[:, :, :, :2].reshape(B, past, 2, C_HD).astype(nsa_rows.dtype)
    kc, vc = nsa_compress(jnp.concatenate([cmp_past, nsa_rows[:, :, :2]], axis=1), lp)
    n_blocks = -(-T // SEL_BLOCK)
    n_past_blk = past // SEL_BLOCK
    n_new_blk = n_blocks - n_past_blk
    bpp = page // SEL_BLOCK
    pool_blocks = pool_n.reshape(pool_n.shape[0] * bpp, SEL_BLOCK, 4, C_HD)
    new_sel = jnp.pad(nsa_rows[:, :, 2:], ((0, 0), (0, n_new_blk * SEL_BLOCK - L), (0, 0), (0, 0)))
    new_sel = new_sel.reshape(B, n_new_blk, SEL_BLOCK, 2, C_HD)

    def gather_blocks(idx):
        jp = jnp.minimum(idx, n_past_blk - 1)
        phys_page = jax.vmap(lambda pt, j: pt[j])(page_table, jp // bpp)
        from_past = jnp.take(pool_blocks, phys_page * bpp + jp % bpp, axis=0)[..., 2:, :]
        from_new = jax.vmap(lambda blks, i: blks[i])(new_sel, jnp.clip(idx - n_past_blk, 0, n_new_blk - 1))
        return jnp.where((idx < n_past_blk)[..., None, None, None], from_past.astype(from_new.dtype), from_new)

    wlen = win_state.shape[1]
    kw = jnp.concatenate([win_state.astype(win_rows.dtype), win_rows], axis=1)
    wpos = past - wlen + jnp.arange(wlen + L, dtype=jnp.int32)
    o_nsa = nsa_core(cq, qpos, gates, kc, vc, gather_blocks, n_blocks, kw[:, :, 0], kw[:, :, 1], wpos)
    out = mix_out(o_diff, o_ssm, o_nsa, lp['w_out'])
    return out, diff_rows, nsa_rows, kw[:, L:], ssm_f.astype(ssm_state.dtype), conv_f


def setup_inputs(seed: int = 0) -> dict:
    key = jax.random.key(seed)
    ks = jax.random.split(key, 40)
    f32 = jnp.float32
    n_pages = PAST_LEN // PAGE_SIZE
    n_used = DEC_BATCH * n_pages
    n_pool = n_used + max(1, n_used // 4)
    win_len = min(WINDOW, PAST_LEN)
    nrm = lambda k, shape, s: jax.random.normal(k, shape, f32) * s
    gain = lambda k, shape: 1.0 + 0.01 * jax.random.normal(k, shape, f32)
    dt0 = jnp.exp(jax.random.uniform(ks[20], (DEPTH, M_HEADS), f32) * (math.log(0.1) - math.log(0.001))
                  + math.log(0.001))
    return {
        'x_prompt': nrm(ks[0], (BATCH, SEQ, D_MODEL), 1.0),
        'x_sample': nrm(ks[1], (DEC_BATCH, DEC_SEQ, D_MODEL), 1.0),
        'cache_diff_kv': nrm(ks[2], (DEPTH, n_pool, PAGE_SIZE, 2, A_HEADS, A_HD), 1.0),
        'cache_nsa_kv': nrm(ks[3], (DEPTH, n_pool, PAGE_SIZE, 4, C_HD), 1.0),
        'state_nsa_win': nrm(ks[4], (DEPTH, DEC_BATCH, win_len, 2, C_HD), 1.0),
        'state_ssm': nrm(ks[5], (DEPTH, DEC_BATCH, M_HEADS, M_HD, M_STATE), 0.5),
        'state_conv': nrm(ks[6], (DEPTH, DEC_BATCH, CONV_W - 1, M_CONV_DIM), 1.0),
        'page_table': jax.random.permutation(ks[7], n_pool)[:n_used].reshape(DEC_BATCH, n_pages).astype(jnp.int32),
        'ffn1_norm': gain(ks[8], (DEPTH, D_MODEL)),
        'ffn1_w_gate': nrm(ks[9], (DEPTH, D_MODEL, D_FF), D_MODEL ** -0.5),
        'ffn1_w_up': nrm(ks[10], (DEPTH, D_MODEL, D_FF), D_MODEL ** -0.5),
        'ffn1_w_down': nrm(ks[11], (DEPTH, D_FF, D_MODEL), D_FF ** -0.5),
        'mix_norm': gain(ks[12], (DEPTH, D_MODEL)),
        'w_in': nrm(ks[13], (DEPTH, D_MODEL, D_IN_PROJ), D_MODEL ** -0.5),
        'diff_q_norm': gain(ks[14], (DEPTH, A_QK)),
        'diff_k_norm': gain(ks[15], (DEPTH, A_QK)),
        'diff_lambda': nrm(ks[16], (DEPTH, 4, A_QK), 0.1),
        'diff_subln': gain(ks[17], (DEPTH, A_HD)),
        'conv_w': nrm(ks[18], (DEPTH, CONV_W, M_CONV_DIM), 0.5),
        'conv_b': nrm(ks[19], (DEPTH, M_CONV_DIM), 0.01),
        'dt_bias': dt0 + jnp.log(-jnp.expm1(-dt0)),
        'a_log': jnp.log(jax.random.uniform(ks[21], (DEPTH, M_HEADS), f32, 1.0, 16.0)),
        'd_skip': 1.0 + 0.1 * jax.random.normal(ks[22], (DEPTH, M_HEADS), f32),
        'ssm_norm': gain(ks[23], (DEPTH, M_INNER)),
        'nsa_q_norm': gain(ks[24], (DEPTH, C_HD)),
        'nsa_k_norm': gain(ks[25], (DEPTH, 3, C_HD)),
        'nsa_cmp_pe': nrm(ks[26], (DEPTH, CMP_BLOCK, 2, C_HD), 0.1),
        'nsa_cmp_w': nrm(ks[27], (DEPTH, 2, CMP_BLOCK, C_HD, C_HD), (CMP_BLOCK * C_HD) ** -0.5),
        'w_out': nrm(ks[28], (DEPTH, D_MIX, D_MODEL), D_MIX ** -0.5),
        'ffn2_norm': gain(ks[29], (DEPTH, D_MODEL)),
        'ffn2_w_gate': nrm(ks[30], (DEPTH, D_MODEL, D_FF), D_MODEL ** -0.5),
        'ffn2_w_up': nrm(ks[31], (DEPTH, D_MODEL, D_FF), D_MODEL ** -0.5),
        'ffn2_w_down': nrm(ks[32], (DEPTH, D_FF, D_MODEL), D_FF ** -0.5),
    }


def reference(x_prompt, x_sample, cache_diff_kv, cache_nsa_kv, state_nsa_win, state_ssm, state_conv,
              page_table, ffn1_norm, ffn1_w_gate, ffn1_w_up, ffn1_w_down, mix_norm, w_in,
              diff_q_norm, diff_k_norm, diff_lambda, diff_subln, conv_w, conv_b, dt_bias, a_log,
              d_skip, ssm_norm, nsa_q_norm, nsa_k_norm, nsa_cmp_pe, nsa_cmp_w, w_out,
              ffn2_norm, ffn2_w_gate, ffn2_w_up, ffn2_w_down):
    xp, xs = x_prompt, x_sample
    p_dkv, p_nkv, p_win, p_ssm, p_conv = [], [], [], [], []
    s_dkv, s_nkv, s_win, s_ssm, s_conv = [], [], [], [], []
    for l in range(DEPTH):
        lam_init = 0.8 - 0.6 * math.exp(-0.3 * l)
        dl = diff_lambda[l].astype(jnp.float32)
        lam = jnp.exp(jnp.sum(dl[0] * dl[1])) - jnp.exp(jnp.sum(dl[2] * dl[3])) + lam_init
        lp = {'w_in': w_in[l], 'diff_q_norm': diff_q_norm[l], 'diff_k_norm': diff_k_norm[l],
              'diff_subln': diff_subln[l], 'conv_w': conv_w[l], 'conv_b': conv_b[l],
              'dt_bias': dt_bias[l], 'a_log': a_log[l], 'd_skip': d_skip[l], 'ssm_norm': ssm_norm[l],
              'nsa_q_norm': nsa_q_norm[l], 'nsa_k_norm': nsa_k_norm[l], 'nsa_cmp_pe': nsa_cmp_pe[l],
              'nsa_cmp_w': nsa_cmp_w[l], 'w_out': w_out[l]}
        xp = xp + 0.5 * swiglu_ffn(xp, ffn1_norm[l], ffn1_w_gate[l], ffn1_w_up[l], ffn1_w_down[l])
        xs = xs + 0.5 * swiglu_ffn(xs, ffn1_norm[l], ffn1_w_gate[l], ffn1_w_up[l], ffn1_w_down[l])
        mp, dkv, nkv, win, ssm, conv = mix_prompt(rmsnorm(xp, mix_norm[l]), lp, lam, lam_init)
        xp = xp + mp
        p_dkv.append(dkv); p_nkv.append(nkv); p_win.append(win); p_ssm.append(ssm); p_conv.append(conv)
        ms, dkv, nkv, win, ssm, conv = mix_sample(rmsnorm(xs, mix_norm[l]), cache_diff_kv[l], cache_nsa_kv[l],
                                                  state_nsa_win[l], state_ssm[l], state_conv[l],
                                                  page_table, lp, lam, lam_init)
        xs = xs + ms
        s_dkv.append(dkv); s_nkv.append(nkv); s_win.append(win); s_ssm.append(ssm); s_conv.append(conv)
        xp = xp + 0.5 * swiglu_ffn(xp, ffn2_norm[l], ffn2_w_gate[l], ffn2_w_up[l], ffn2_w_down[l])
        xs = xs + 0.5 * swiglu_ffn(xs, ffn2_norm[l], ffn2_w_gate[l], ffn2_w_up[l], ffn2_w_down[l])
    return (xp, xs,
            jnp.stack(p_dkv), jnp.stack(p_nkv), jnp.stack(p_win), jnp.stack(p_ssm), jnp.stack(p_conv),
            jnp.stack(s_dkv), jnp.stack(s_nkv), jnp.stack(s_win), jnp.stack(s_ssm), jnp.stack(s_conv))
```

```python
import functools
import math

import jax
import jax.numpy as jnp
import numpy as np
from jax import lax
from jax.experimental import pallas as pl
from jax.experimental.pallas import tpu as pltpu

F32, BF16, I32 = jnp.float32, jnp.bfloat16, jnp.int32

A_HEADS, A_QK, A_HD = 4, 32, 64
M_HEADS, M_HD, M_INNER, M_GROUPS, M_STATE, CONV_W, M_CONV_DIM, M_CHUNK = 8, 64, 512, 2, 128, 4, 1024, 128
C_HEADS, C_HD, CMP_BLOCK, SEL_BLOCK, N_SEL, WINDOW = 4, 64, 64, 64, 16, 512
EPS = 1e-6
NEG = -1e30
LANES = 128
OFF_AQ, OFF_AK, OFF_AV, OFF_Z, OFF_XBC, OFF_CQ, OFF_CKV, OFF_DTG, W_IN_COLS = 0, 256, 512, 768, 1280, 2304, 2560, 2944, 3072
N_DT = M_HEADS
N_GATE = 3 * C_HEADS
ROW_TILE = 512
ATTN_TQ, ATTN_TK = 256, 512
DECODE_PAGES_PER_STEP = 8


def _cparams(sem, vmem_mib):
    return pltpu.CompilerParams(dimension_semantics=sem, vmem_limit_bytes=vmem_mib * 1024 * 1024)


def _dot(a, b):
    return jnp.dot(a, b, preferred_element_type=F32)


def _dot_nt(a, b):
    return lax.dot_general(a, b, (((1,), (1,)), ((), ())), preferred_element_type=F32)


def _split3(x):
    x1 = x.astype(BF16)
    r = x - x1.astype(F32)
    x2 = r.astype(BF16)
    x3 = (r - x2.astype(F32)).astype(BF16)
    return x1, x2, x3


def _dot_exact_rhs(x, a_bf):
    x1, x2, x3 = _split3(x)
    return _dot(x1, a_bf) + _dot(x2, a_bf) + _dot(x3, a_bf)


def _dot_exact_lhs(a_bf, x):
    x1, x2, x3 = _split3(x)
    return _dot(a_bf, x1) + _dot(a_bf, x2) + _dot(a_bf, x3)


def _sigmoid(x):
    return 1.0 / (1.0 + jnp.exp(-x))


def _silu(x):
    return x * _sigmoid(x)


def _softplus(x):
    return jnp.maximum(x, 0.0) + jnp.log1p(jnp.exp(-jnp.abs(x)))


def _rms_rows(x, g):
    return x * lax.rsqrt(jnp.mean(x * x, axis=-1, keepdims=True) + EPS) * g


def _group_norm(v, gmat_bf):
    sq = v * v
    hi = sq.astype(BF16)
    lo = (sq - hi.astype(F32)).astype(BF16)
    ms = _dot(hi, gmat_bf) + _dot(lo, gmat_bf)
    return v * lax.rsqrt(ms + EPS)


def _ffn_kernel(x_ref, g_ref, wg_ref, wu_ref, wd_ref, o_ref, h_sc, acc_sc):
    c = pl.program_id(1)

    @pl.when(c == 0)
    def _():
        h_sc[...] = _rms_rows(x_ref[...], g_ref[...]).astype(BF16)
        acc_sc[...] = jnp.zeros_like(acc_sc)

    h = h_sc[...]
    a = _dot(h, wg_ref[...])
    b = _dot(h, wu_ref[...])
    acc_sc[...] += _dot((_silu(a) * b).astype(BF16), wd_ref[...])

    @pl.when(c == pl.num_programs(1) - 1)
    def _():
        o_ref[...] = x_ref[...] + 0.5 * acc_sc[...]


def _ffn(x, g, wg, wu, wd, tm):
    m, d = x.shape
    ff = wg.shape[1]
    fc = ff // 2
    return pl.pallas_call(
        _ffn_kernel,
        grid=(m // tm, ff // fc),
        in_specs=[
            pl.BlockSpec((tm, d), lambda i, c: (i, 0)),
            pl.BlockSpec((1, d), lambda i, c: (0, 0)),
            pl.BlockSpec((d, fc), lambda i, c: (0, c)),
            pl.BlockSpec((d, fc), lambda i, c: (0, c)),
            pl.BlockSpec((fc, d), lambda i, c: (c, 0)),
        ],
        out_specs=pl.BlockSpec((tm, d), lambda i, c: (i, 0)),
        out_shape=jax.ShapeDtypeStruct((m, d), F32),
        scratch_shapes=[pltpu.VMEM((tm, d), BF16), pltpu.VMEM((tm, d), F32)],
        compiler_params=_cparams(("parallel", "arbitrary"), 56),
        name="ffn",
    )(x, g, wg, wu, wd)


def _inproj_kernel(x_ref, g_ref, w_ref, g32_ref, g64_ref, qg_ref, kg_ref, cqg_ref, ng_ref, nm_ref, wg_ref, wm_ref,
                   qd_ref, dkv_ref, dkvb_ref, z_ref, xbc_ref, dtg_ref, cq_ref, nsa_ref, nsab_ref,
                   win_ref, winb_ref):
    h = _rms_rows(x_ref[...], g_ref[...]).astype(BF16)

    def seg(a, b):
        return _dot(h, w_ref[:, a:b])

    g32 = g32_ref[...]
    g64 = g64_ref[...]
    qd_ref[...] = (_group_norm(seg(OFF_AQ, OFF_AK), g32) * qg_ref[...]).astype(BF16)
    kn = _group_norm(seg(OFF_AK, OFF_AV), g32) * kg_ref[...]
    av = seg(OFF_AV, OFF_Z)
    dkv_ref[:, 0:256] = kn
    dkv_ref[:, 256:512] = av
    dkvb_ref[:, 0:256] = kn.astype(BF16)
    dkvb_ref[:, 256:512] = av.astype(BF16)
    z_ref[...] = seg(OFF_Z, OFF_XBC)
    xbc_ref[...] = seg(OFF_XBC, OFF_CQ)
    cq_ref[...] = (_group_norm(seg(OFF_CQ, OFF_CKV), g64) * cqg_ref[...]).astype(BF16)
    nr = seg(OFF_CKV, OFF_CKV + 256)
    nsa = jnp.where(nm_ref[...] > 0.5, _group_norm(nr, g64) * ng_ref[...], nr)
    nsa_ref[...] = nsa
    nsab_ref[...] = nsa.astype(BF16)
    wr = seg(OFF_CKV + 256, OFF_DTG)
    win = jnp.where(wm_ref[...] > 0.5, _group_norm(wr, g64_ref[0:128, 0:128]) * wg_ref[...], wr)
    win_ref[...] = win
    winb_ref[...] = win.astype(BF16)
    d = seg(OFF_DTG, W_IN_COLS)
    lane = lax.broadcasted_iota(I32, d.shape, 1)
    dtg_ref[...] = jnp.where(lane >= N_DT, _sigmoid(d), d)


def _inproj(x, p, tm):
    m, d = x.shape
    row = lambda w: pl.BlockSpec((tm, w), lambda i: (i, 0))
    full = lambda a: pl.BlockSpec(a.shape, lambda i: (0,) * a.ndim)
    consts = [p["mix_norm"], p["w_in"], p["g32"], p["g64"], p["qg"], p["kg"], p["cqg"], p["ng"], p["nm"], p["wg"], p["wm"]]
    outs = [(256, BF16), (512, F32), (512, BF16), (512, F32), (1024, F32), (128, F32), (256, BF16),
            (256, F32), (256, BF16), (128, F32), (128, BF16)]
    return pl.pallas_call(
        _inproj_kernel,
        grid=(m // tm,),
        in_specs=[row(d)] + [full(a) for a in consts],
        out_specs=[row(w) for w, _ in outs],
        out_shape=[jax.ShapeDtypeStruct((m, w), dt) for w, dt in outs],
        compiler_params=_cparams(("parallel",), 56),
        name="inproj",
    )(x, *consts)


def _mixout_kernel(x_ref, od_ref, os_ref, on_ref, w_ref, o_ref):
    o_ref[...] = (x_ref[...]
                  + _dot(od_ref[...].astype(BF16), w_ref[0:256, :])
                  + _dot(os_ref[...].astype(BF16), w_ref[256:768, :])
                  + _dot(on_ref[...].astype(BF16), w_ref[768:1024, :]))


def _mixout(x, od, osm, on, w, tm):
    m, d = x.shape
    row = lambda w_: pl.BlockSpec((tm, w_), lambda i: (i, 0))
    return pl.pallas_call(
        _mixout_kernel,
        grid=(m // tm,),
        in_specs=[row(d), row(256), row(512), row(256), pl.BlockSpec(w.shape, lambda i: (0, 0))],
        out_specs=row(d),
        out_shape=jax.ShapeDtypeStruct((m, d), F32),
        compiler_params=_cparams(("parallel",), 40),
        name="mixout",
    )(x, od, osm, on, w)


IN_SPLITS = (256, 256, 256, 512, 1024, 8, 256, 384, 12)


def _block_diag_mean(width, group):
    m = np.zeros((width, width), np.float32)
    for s in range(0, width, group):
        m[s:s + group, s:s + group] = 1.0 / group
    return jnp.asarray(m, BF16)


def _prep_layer(l, P):
    d = P["w_in"].shape[1]
    offs = np.cumsum((0,) + IN_SPLITS)
    cols = [P["w_in"][l][:, offs[i]:offs[i + 1]] for i in range(len(IN_SPLITS))]
    aq, ak, av, z, xbc, dt, cq, ckv, cg = cols
    cg = cg.reshape(d, C_HEADS, 3).transpose(0, 2, 1).reshape(d, N_GATE)
    dtg = jnp.concatenate([dt, cg, jnp.zeros((d, LANES - N_DT - N_GATE), F32)], axis=1)
    w_in = jnp.concatenate([aq, ak, av, z, xbc, cq, ckv, dtg], axis=1).astype(BF16)
    ones64 = jnp.ones((C_HD,), F32)
    kn = P["nsa_k_norm"][l]
    p = {
        "mix_norm": P["mix_norm"][l][None, :],
        "w_in": w_in,
        "g32": _block_diag_mean(256, A_QK),
        "g64": _block_diag_mean(256, C_HD),
        "qg": (jnp.tile(P["diff_q_norm"][l], 2 * A_HEADS) * (A_QK ** -0.5))[None, :],
        "kg": jnp.tile(P["diff_k_norm"][l], 2 * A_HEADS)[None, :],
        "cqg": (jnp.tile(P["nsa_q_norm"][l], C_HEADS) * (C_HD ** -0.5))[None, :],
        "ng": jnp.concatenate([ones64, ones64, kn[1], ones64])[None, :],
        "nm": jnp.concatenate([0 * ones64, 0 * ones64, ones64, 0 * ones64])[None, :],
        "wg": jnp.concatenate([kn[2], ones64])[None, :],
        "wm": jnp.concatenate([ones64, 0 * ones64])[None, :],
    }
    for f in ("ffn1", "ffn2"):
        p[f] = (P[f + "_norm"][l][None, :], P[f + "_w_gate"][l].astype(BF16), P[f + "_w_up"][l].astype(BF16),
                P[f + "_w_down"][l].astype(BF16))
    p["w_out"] = P["w_out"][l].astype(BF16)
    return p


def _causal_pairs(n_q, tq, tk, window=None):
    qi, kj, lf, wj = [], [], [], []
    for i in range(n_q):
        last = (i * tq + tq - 1) // tk
        first_w = max(0, (i * tq - (window or 0)) // tk)
        for j in range(last + 1):
            qi.append(i), kj.append(j), lf.append(int(j == last)), wj.append(max(j, first_w))
    return tuple(jnp.asarray(np.asarray(a, np.int32)) for a in (qi, kj, lf, wj))


def _diff_lambda(dl, lam_init):
    a = jnp.sum(dl[0:1] * dl[1:2], axis=-1, keepdims=True)
    b = jnp.sum(dl[2:3] * dl[3:4], axis=-1, keepdims=True)
    return jnp.exp(a) - jnp.exp(b) + lam_init


def _diff_prompt_kernel(qi_ref, kj_ref, lf_ref, q_ref, k_ref, v_ref, dl_ref, sg_ref, o_ref,
                        qm_sc, m_sc, l_sc, acc_sc, *, tq, tk, lam_init):
    t = pl.program_id(0)
    i = qi_ref[t]
    j = kj_ref[t]
    n_hc = 2 * A_HEADS

    @pl.when(j == 0)
    def _():
        q = q_ref[...]
        lane = lax.broadcasted_iota(I32, q.shape, 1)
        for hc in range(n_hc):
            qm_sc[hc] = jnp.where((lane >= A_QK * hc) & (lane < A_QK * (hc + 1)), q, jnp.zeros_like(q))
        m_sc[...] = jnp.full(m_sc.shape, NEG, F32)
        l_sc[...] = jnp.zeros_like(l_sc)
        acc_sc[...] = jnp.zeros_like(acc_sc)

    def sweep(masked):
        k = k_ref[...]
        if masked:
            qpos = i * tq + lax.broadcasted_iota(I32, (tq, tk), 0)
            kpos = j * tk + lax.broadcasted_iota(I32, (tq, tk), 1)
            keep = kpos <= qpos
        for hc in range(n_hc):
            s = _dot_nt(qm_sc[hc], k)
            if masked:
                s = jnp.where(keep, s, NEG)
            m_prev = m_sc[hc]
            m_new = jnp.maximum(m_prev, jnp.max(s, axis=-1, keepdims=True))
            alpha = jnp.exp(m_prev - m_new)
            p = jnp.exp(s - m_new)
            l_sc[hc] = alpha * l_sc[hc] + jnp.sum(p, axis=-1, keepdims=True)
            acc_sc[hc] = alpha * acc_sc[hc] + _dot(p.astype(BF16), v_ref[hc // 2])
            m_sc[hc] = m_new

    diag = (j * tk + tk - 1) > (i * tq)
    pl.when(diag)(lambda: sweep(True))
    pl.when(jnp.logical_not(diag))(lambda: sweep(False))

    @pl.when(lf_ref[t] == 1)
    def _():
        lam = _diff_lambda(dl_ref[...], lam_init)
        for h in range(A_HEADS):
            o = acc_sc[2 * h] / l_sc[2 * h] - lam * (acc_sc[2 * h + 1] / l_sc[2 * h + 1])
            o_ref[h] = _rms_rows(o, sg_ref[...]) * (1.0 - lam_init)


def _diff_prompt(qd, dkvb, dl, subg, lam_init, tq, tk):
    s = qd.shape[0]
    assert tk % tq == 0 and s % tk == 0
    vhm = dkvb[:, 256:].reshape(s, A_HEADS, A_HD).transpose(1, 0, 2)
    qi, kj, lf, _ = _causal_pairs(s // tq, tq, tk)
    grid_spec = pltpu.PrefetchScalarGridSpec(
        num_scalar_prefetch=3,
        grid=(qi.shape[0],),
        in_specs=[
            pl.BlockSpec((tq, 256), lambda t, qi, kj, lf: (qi[t], 0)),
            pl.BlockSpec((tk, 256), lambda t, qi, kj, lf: (kj[t], 0)),
            pl.BlockSpec((A_HEADS, tk, A_HD), lambda t, qi, kj, lf: (0, kj[t], 0)),
            pl.BlockSpec((4, A_QK), lambda t, qi, kj, lf: (0, 0)),
            pl.BlockSpec((1, A_HD), lambda t, qi, kj, lf: (0, 0)),
        ],
        out_specs=pl.BlockSpec((A_HEADS, tq, A_HD), lambda t, qi, kj, lf: (0, qi[t], 0)),
        scratch_shapes=[pltpu.VMEM((2 * A_HEADS, tq, 256), BF16), pltpu.VMEM((2 * A_HEADS, tq, 1), F32),
                        pltpu.VMEM((2 * A_HEADS, tq, 1), F32), pltpu.VMEM((2 * A_HEADS, tq, A_HD), F32)],
    )
    o = pl.pallas_call(
        functools.partial(_diff_prompt_kernel, tq=tq, tk=tk, lam_init=lam_init),
        grid_spec=grid_spec,
        out_shape=jax.ShapeDtypeStruct((A_HEADS, s, A_HD), F32),
        compiler_params=_cparams(("arbitrary",), 48),
        name="diff_prompt",
    )(qi, kj, lf, qd, dkvb, vhm, dl, subg)
    return o.transpose(1, 0, 2).reshape(s, A_HEADS * A_HD)


def _diff_decode_kernel(pt_ref, q_ref, new_ref, *rest, n_pg, lam_init):
    page_refs = rest[:n_pg]
    dl_ref, sg_ref, o_ref, m_sc, l_sc, acc_sc = rest[n_pg:]
    step = pl.program_id(1)
    n_hc = 2 * A_HEADS
    lane = lax.broadcasted_iota(I32, (n_hc, 256), 1)
    row = lax.broadcasted_iota(I32, (n_hc, 256), 0)
    qm = jnp.where((lane >= A_QK * row) & (lane < A_QK * (row + 1)), q_ref[0], 0.0)

    @pl.when(step == 0)
    def _():
        new = new_ref[0]
        m_sc[...] = jnp.sum(qm * new[:, 0:256], axis=-1, keepdims=True)
        l_sc[...] = jnp.ones_like(l_sc)
        acc_sc[...] = jnp.broadcast_to(new[:, 256:512], acc_sc.shape)

    k = jnp.concatenate([r[:, 0:256] for r in page_refs], axis=0)
    v = jnp.concatenate([r[:, 256:512] for r in page_refs], axis=0)
    s = _dot_nt(qm.astype(BF16), k)
    m_prev = m_sc[...]
    m_new = jnp.maximum(m_prev, jnp.max(s, axis=-1, keepdims=True))
    alpha = jnp.exp(m_prev - m_new)
    p = jnp.exp(s - m_new)
    l_sc[...] = alpha * l_sc[...] + jnp.sum(p, axis=-1, keepdims=True)
    acc_sc[...] = alpha * acc_sc[...] + _dot(p.astype(BF16), v)
    m_sc[...] = m_new

    @pl.when(step == pl.num_programs(1) - 1)
    def _():
        lam = _diff_lambda(dl_ref[...], lam_init)
        a = acc_sc[...] / l_sc[...]
        lane1 = lax.broadcasted_iota(I32, (1, 256), 1)
        o = jnp.zeros((1, 256), F32)
        for h in range(A_HEADS):
            in_h = (lane1 >= A_HD * h) & (lane1 < A_HD * (h + 1))
            oh = a[2 * h:2 * h + 1, :] - lam * a[2 * h + 1:2 * h + 2, :]
            ms = jnp.sum(jnp.where(in_h, oh * oh, 0.0), axis=-1, keepdims=True) * (1.0 / A_HD)
            o = jnp.where(in_h, oh * lax.rsqrt(ms + EPS), o)
        o_ref[0] = o * sg_ref[...] * (1.0 - lam_init)


def _diff_decode(qd, dkv_new, pool_bf, page_table, dl, subg4, lam_init, n_pg):
    b, n_pages = page_table.shape
    page = pool_bf.shape[1]
    assert n_pages % n_pg == 0
    cmap = lambda bi, si, pt: (0, 0)
    page_specs = [pl.BlockSpec((None, page, 512), functools.partial(lambda bi, si, pt, g: (pt[bi, si * n_pg + g], 0, 0), g=g))
                  for g in range(n_pg)]
    grid_spec = pltpu.PrefetchScalarGridSpec(
        num_scalar_prefetch=1,
        grid=(b, n_pages // n_pg),
        in_specs=[pl.BlockSpec((1, 1, 256), lambda bi, si, pt: (bi, 0, 0)),
                  pl.BlockSpec((1, 1, 512), lambda bi, si, pt: (bi, 0, 0))] + page_specs +
                 [pl.BlockSpec((4, A_QK), cmap), pl.BlockSpec((1, 256), cmap)],
        out_specs=pl.BlockSpec((1, 1, 256), lambda bi, si, pt: (bi, 0, 0)),
        scratch_shapes=[pltpu.VMEM((2 * A_HEADS, 1), F32), pltpu.VMEM((2 * A_HEADS, 1), F32),
                        pltpu.VMEM((2 * A_HEADS, 256), F32)],
    )
    o = pl.pallas_call(
        functools.partial(_diff_decode_kernel, n_pg=n_pg, lam_init=lam_init),
        grid_spec=grid_spec,
        out_shape=jax.ShapeDtypeStruct((b, 1, 256), F32),
        compiler_params=_cparams(("parallel", "arbitrary"), 48),
        name="diff_decode",
    )(page_table, qd.astype(F32).reshape(b, 1, 256), dkv_new.reshape(b, 1, 512), *([pool_bf] * n_pg), dl, subg4)
    return o.reshape(b, 256)


def _ssd_consts():
    tri = np.tril(np.ones((M_CHUNK, M_CHUNK), np.float32))
    expand = np.zeros((LANES, M_INNER), np.float32)
    for h in range(M_HEADS):
        expand[h, h * M_HD:(h + 1) * M_HD] = 1.0
    return jnp.asarray(tri, BF16), jnp.asarray(tri.T, BF16), jnp.asarray(expand, BF16)


def _ssd_params(P, l):
    pad = lambda v: jnp.concatenate([v, jnp.zeros((LANES - M_HEADS,), F32)])[None, :]
    return dict(cw=P["conv_w"][l], cb=P["conv_b"][l][None, :], dtb=pad(P["dt_bias"][l]), dtbT=P["dt_bias"][l][:, None],
                alog=pad(P["a_log"][l]), alogT=P["a_log"][l][:, None],
                dskip=jnp.repeat(P["d_skip"][l], M_HD)[None, :], nrm=P["ssm_norm"][l][None, :])


def _gated_group_norm(y, x, z, dskip, nrm):
    g = (y + x * dskip) * _silu(z)
    gw = M_INNER // M_GROUPS
    return jnp.concatenate([_rms_rows(g[:, i * gw:(i + 1) * gw], nrm[:, i * gw:(i + 1) * gw]) for i in range(M_GROUPS)], axis=1)


def _ssd_prompt_kernel(z_ref, xbc_ref, dtg_ref, dtT_ref, cw_ref, cb_ref, dtb_ref, dtbT_ref, alog_ref, alogT_ref,
                       dskip_ref, nrm_ref, tri_ref, triT_ref, ex_ref, y_ref, hT_ref, xpad_sc, h_sc):
    c = pl.program_id(0)
    q = M_CHUNK

    @pl.when(c == 0)
    def _():
        xpad_sc[0:8, :] = jnp.zeros((8, M_CONV_DIM), F32)
        h_sc[...] = jnp.zeros_like(h_sc)

    xpad_sc[8:8 + q, :] = xbc_ref[...]
    conv = cb_ref[...]
    for j in range(CONV_W):
        conv = conv + cw_ref[j:j + 1, :] * xpad_sc[8 - (CONV_W - 1) + j:8 - (CONV_W - 1) + j + q, :]
    xpad_sc[0:8, :] = xbc_ref[q - 8:q, :]
    xa = _silu(conv)
    x = xa[:, 0:M_INNER]
    bm = xa[:, M_INNER:M_INNER + M_GROUPS * M_STATE]
    cm = xa[:, M_INNER + M_GROUPS * M_STATE:]

    lane = lax.broadcasted_iota(I32, (1, LANES), 1)
    is_head = lane < M_HEADS
    a_row = jnp.where(is_head, -jnp.exp(alog_ref[...]), 0.0)
    dt = jnp.where(is_head, _softplus(dtg_ref[...] + dtb_ref[...]), 0.0)
    acum = _dot_exact_lhs(tri_ref[...], dt * a_row)
    daT = _softplus(dtT_ref[...] + dtbT_ref[...]) * (-jnp.exp(alogT_ref[...]))
    acumT = _dot_exact_rhs(daT, triT_ref[...])
    ex = ex_ref[...]
    acum_x = _dot_exact_rhs(acum, ex)
    dt_x = _dot_exact_rhs(dt, ex)
    last = acum_x[q - 1:q, :]
    xdt = x * dt_x
    xdd = (xdt * jnp.exp(last - acum_x)).astype(BF16)
    h_prev = h_sc[...]

    rowi = lax.broadcasted_iota(I32, (q, q), 0)
    coli = lax.broadcasted_iota(I32, (q, q), 1)
    causal = rowi >= coli
    lane_q = lax.broadcasted_iota(I32, (q, LANES), 1)
    hpg = M_HEADS // M_GROUPS
    gw = hpg * M_HD
    y_diag, y_off, st = [], [], []
    for g in range(M_GROUPS):
        c_g = cm[:, g * M_STATE:(g + 1) * M_STATE].astype(BF16)
        bT_g = bm[:, g * M_STATE:(g + 1) * M_STATE].T.astype(BF16)
        y_off.append(_dot(c_g, h_prev[:, g * gw:(g + 1) * gw].astype(BF16)))
        st.append(_dot(bT_g, xdd[:, g * gw:(g + 1) * gw]))
        cb = _dot(c_g, bT_g)
        for pr in range(hpg // 2):
            ms = []
            for hh in range(2):
                h = g * hpg + pr * 2 + hh
                seg = acum[:, h:h + 1] - acumT[h:h + 1, :]
                ms.append(cb * jnp.where(causal, jnp.exp(jnp.where(causal, seg, 0.0)), 0.0))
            xp = xdt[:, (g * hpg + pr * 2) * M_HD:(g * hpg + pr * 2 + 2) * M_HD]
            rhs = jnp.concatenate([jnp.where(lane_q < M_HD, xp, 0.0), jnp.where(lane_q >= M_HD, xp, 0.0)], axis=0)
            y_diag.append(_dot(jnp.concatenate(ms, axis=1).astype(BF16), rhs.astype(BF16)))
    y = jnp.concatenate(y_diag, axis=1) + jnp.concatenate(y_off, axis=1) * jnp.exp(acum_x)
    h_new = h_prev * jnp.exp(last) + jnp.concatenate(st, axis=1)
    h_sc[...] = h_new
    y_ref[...] = _gated_group_norm(y, x, z_ref[...], dskip_ref[...], nrm_ref[...])

    @pl.when(c == pl.num_programs(0) - 1)
    def _():
        hT_ref[...] = h_new


def _ssd_prompt(z, xbc, dtg, sp):
    s = z.shape[0]
    q = M_CHUNK
    tri, triT, ex = _ssd_consts()
    dtT = dtg[:, :M_HEADS].T
    consts = [sp["cw"], sp["cb"], sp["dtb"], sp["dtbT"], sp["alog"], sp["alogT"], sp["dskip"], sp["nrm"], tri, triT, ex]
    row = lambda w: pl.BlockSpec((q, w), lambda c: (c, 0))
    y, hT = pl.pallas_call(
        _ssd_prompt_kernel,
        grid=(s // q,),
        in_specs=[row(M_INNER), row(M_CONV_DIM), row(LANES), pl.BlockSpec((M_HEADS, q), lambda c: (0, c))]
                 + [pl.BlockSpec(a.shape, lambda c: (0, 0)) for a in consts],
        out_specs=[row(M_INNER), pl.BlockSpec((M_STATE, M_INNER), lambda c: (0, 0))],
        out_shape=[jax.ShapeDtypeStruct((s, M_INNER), F32), jax.ShapeDtypeStruct((M_STATE, M_INNER), F32)],
        scratch_shapes=[pltpu.VMEM((8 + q, M_CONV_DIM), F32), pltpu.VMEM((M_STATE, M_INNER), F32)],
        compiler_params=_cparams(("arbitrary",), 40),
        name="ssd_prompt",
    )(z, xbc, dtg, dtT, *consts)
    return y, hT.reshape(M_STATE, M_HEADS, M_HD).transpose(1, 2, 0)


def _ssd_decode_kernel(z_ref, xbc_ref, dtg_ref, cs_ref, st_ref, cw_ref, cb_ref, dtb_ref, alog_ref, dskip_ref, nrm_ref,
                       ex_ref, y_ref, sto_ref, cso_ref):
    b = z_ref.shape[0]
    xbc = xbc_ref[...]
    conv = cb_ref[...] + cw_ref[CONV_W - 1:CONV_W, :] * xbc
    for j in range(CONV_W - 1):
        conv = conv + cw_ref[j:j + 1, :] * cs_ref[j]
        if j > 0:
            cso_ref[j - 1] = cs_ref[j]
    cso_ref[CONV_W - 2] = xbc
    xa = _silu(conv)
    x = xa[:, 0:M_INNER]
    bm = xa[:, M_INNER:M_INNER + M_GROUPS * M_STATE]
    cm = xa[:, M_INNER + M_GROUPS * M_STATE:]
    lane = lax.broadcasted_iota(I32, (1, LANES), 1)
    is_head = lane < M_HEADS
    dt = jnp.where(is_head, _softplus(dtg_ref[...] + dtb_ref[...]), 0.0)
    da = dt * jnp.where(is_head, -jnp.exp(alog_ref[...]), 0.0)
    ex = ex_ref[...]
    xdt = x * _dot_exact_rhs(dt, ex)
    dec = jnp.exp(_dot_exact_rhs(da, ex))
    stack = jnp.concatenate([xdt, dec, jnp.zeros((LANES - 2 * b, M_INNER), F32)], axis=0)
    cols = stack.T
    lane_c = lax.broadcasted_iota(I32, (M_INNER, LANES), 1)
    ycols = jnp.zeros((M_INNER, LANES), F32)
    gw = M_INNER // M_GROUPS
    for s in range(b):
        xcol = cols[:, s:s + 1]
        dcol = cols[:, b + s:b + s + 1]
        h = st_ref[s]
        parts = []
        for g in range(M_GROUPS):
            rows = slice(g * gw, (g + 1) * gw)
            hn = h[rows] * dcol[rows] + xcol[rows] * bm[s:s + 1, g * M_STATE:(g + 1) * M_STATE]
            sto_ref[s, rows, :] = hn
            parts.append(jnp.sum(hn * cm[s:s + 1, g * M_STATE:(g + 1) * M_STATE], axis=-1, keepdims=True))
        ycols = jnp.where(lane_c == s, jnp.concatenate(parts, axis=0), ycols)
    y = ycols.T[0:b, :]
    y_ref[...] = _gated_group_norm(y, x, z_ref[...], dskip_ref[...], nrm_ref[...])


def _ssd_decode(z, xbc, dtg, conv_state, ssm_state, sp):
    b = z.shape[0]
    assert 2 * b <= LANES and b % 8 == 0
    _, _, ex = _ssd_consts()
    y, st, cs = pl.pallas_call(
        _ssd_decode_kernel,
        out_shape=[jax.ShapeDtypeStruct((b, M_INNER), F32), jax.ShapeDtypeStruct((b, M_INNER, M_STATE), F32),
                   jax.ShapeDtypeStruct((CONV_W - 1, b, M_CONV_DIM), F32)],
        compiler_params=pltpu.CompilerParams(vmem_limit_bytes=48 * 1024 * 1024),
        name="ssd_decode",
    )(z, xbc, dtg, conv_state.transpose(1, 0, 2), ssm_state.reshape(b, M_INNER, M_STATE),
      sp["cw"], sp["cb"], sp["dtb"], sp["alog"], sp["dskip"], sp["nrm"], ex)
    return y, st.reshape(b, M_HEADS, M_HD, M_STATE), cs.transpose(1, 0, 2)


CMP_K = CMP_BLOCK * 2 * C_HD


def _cmp_params(P, l):
    w = P["nsa_cmp_w"][l]
    wk = jnp.concatenate([w[0], jnp.zeros_like(w[0])], axis=-1)
    wv = jnp.concatenate([jnp.zeros_like(w[1]), w[1]], axis=-1)
    wc = jnp.stack([wk, wv], axis=1).reshape(CMP_K, 2 * C_HD).astype(BF16)
    pe = jnp.broadcast_to(P["nsa_cmp_pe"][l].reshape(1, CMP_K), (8, CMP_K))
    kn = jnp.concatenate([P["nsa_k_norm"][l][0], jnp.ones((C_HD,), F32)])[None, :]
    return wc, pe, kn


def _compress_kernel(x_ref, pe_ref, w_ref, g64_ref, kn_ref, o_ref):
    w = w_ref[...]
    pe = pe_ref[...]
    pe_hi = pe.astype(BF16)
    pe_lo = (pe - pe_hi.astype(F32)).astype(BF16)
    bias = (_dot(pe_hi, w) + _dot(pe_lo, w))[0:1, :]
    out = _dot(x_ref[...], w) + bias
    lane = lax.broadcasted_iota(I32, out.shape, 1)
    o_ref[...] = jnp.where(lane < C_HD, _group_norm(out, g64_ref[0:128, 0:128]) * kn_ref[...], out)


def _compress(x_bf, cmp_p, g64, tb):
    nb = x_bf.shape[0]
    wc, pe, kn = cmp_p
    full = lambda a: pl.BlockSpec(a.shape, lambda i: (0, 0))
    return pl.pallas_call(
        _compress_kernel,
        grid=(nb // tb,),
        in_specs=[pl.BlockSpec((tb, CMP_K), lambda i: (i, 0)), full(pe), full(wc), full(g64), full(kn)],
        out_specs=pl.BlockSpec((tb, 2 * C_HD), lambda i: (i, 0)),
        out_shape=jax.ShapeDtypeStruct((nb, 2 * C_HD), F32),
        compiler_params=_cparams(("parallel",), 48),
        name="nsa_compress",
    )(x_bf, pe, wc, g64, kn)


def _pick_top(vals, ids, n_pick, on_pick):
    def body(it, carry):
        vals, state = carry
        m = jnp.max(vals, axis=-1, keepdims=True)
        first = jnp.min(jnp.where(vals == m, ids, 1e9), axis=-1, keepdims=True)
        pick = (ids == first) & (m >= 0.0)
        return jnp.where(pick, -1.0, vals), on_pick(it, pick, state)

    return body


def _nsa_select_kernel(q_ref, kcv_ref, ocmp_ref, sel_ref, *, tq):
    i = pl.program_id(0)
    kcv = kcv_ref[...].astype(BF16)
    nb = kcv.shape[0]
    blk = lax.broadcasted_iota(I32, (tq, nb), 1)
    pos = i * tq + lax.broadcasted_iota(I32, (tq, nb), 0)
    cmask = (blk + 1) * CMP_BLOCK - 1 <= pos
    imp = jnp.zeros((tq, nb), F32)
    for h in range(C_HEADS):
        s = jnp.where(cmask, _dot_nt(q_ref[h], kcv), NEG)
        e = jnp.where(cmask, jnp.exp(s - jnp.max(s, axis=-1, keepdims=True)), 0.0)
        den = jnp.sum(e, axis=-1, keepdims=True)
        p = e / jnp.where(den > 0.0, den, 1.0)
        ocmp_ref[h] = _dot(p.astype(BF16), kcv)
        imp = imp + p
    cur = lax.shift_right_logical(pos, int(math.log2(SEL_BLOCK)))
    blk_f = blk.astype(F32)
    vals = jnp.where(blk < cur, imp, -1.0)
    sel = jnp.where(blk == cur, 1.0, 0.0)
    body = _pick_top(vals, blk_f, N_SEL - 1, lambda it, pick, sel: jnp.where(pick, 1.0, sel))
    _, sel = lax.fori_loop(0, N_SEL - 1, lambda it, c: body(it, c), (vals, sel))
    sel_ref[...] = sel.astype(BF16)


def _nsa_select(qw, kcv, tq):
    s = qw.shape[1]
    nb = kcv.shape[0]
    return pl.pallas_call(
        functools.partial(_nsa_select_kernel, tq=tq),
        grid=(s // tq,),
        in_specs=[pl.BlockSpec((C_HEADS, tq, LANES), lambda i: (0, i, 0)), pl.BlockSpec((nb, LANES), lambda i: (0, 0))],
        out_specs=[pl.BlockSpec((C_HEADS, tq, LANES), lambda i: (0, i, 0)), pl.BlockSpec((tq, nb), lambda i: (i, 0))],
        out_shape=[jax.ShapeDtypeStruct((C_HEADS, s, LANES), F32), jax.ShapeDtypeStruct((s, nb), BF16)],
        compiler_params=_cparams(("parallel",), 40),
        name="nsa_select",
    )(qw, kcv)


def _flash_update(q, kv, valid, m_ref, l_ref, acc_ref, h):
    s = jnp.where(valid, _dot_nt(q, kv), NEG)
    m_prev = m_ref[h]
    m_new = jnp.maximum(m_prev, jnp.max(s, axis=-1, keepdims=True))
    alpha = jnp.exp(m_prev - m_new)
    p = jnp.where(valid, jnp.exp(s - m_new), 0.0)
    l_ref[h] = alpha * l_ref[h] + jnp.sum(p, axis=-1, keepdims=True)
    acc_ref[h] = alpha * acc_ref[h] + _dot(p.astype(BF16), kv)
    m_ref[h] = m_new


def _nsa_attn_kernel(qi_ref, kj_ref, lf_ref, wj_ref, q_ref, ksv_ref, kwv_ref, sel_ref, ocmp_ref, dtg_ref, o_ref,
                     ms_sc, ls_sc, as_sc, mw_sc, lw_sc, aw_sc, *, tq, tk):
    t = pl.program_id(0)
    i = qi_ref[t]
    j = kj_ref[t]

    @pl.when(j == 0)
    def _():
        for m_sc, l_sc, a_sc in ((ms_sc, ls_sc, as_sc), (mw_sc, lw_sc, aw_sc)):
            m_sc[...] = jnp.full(m_sc.shape, NEG, F32)
            l_sc[...] = jnp.zeros_like(l_sc)
            a_sc[...] = jnp.zeros_like(a_sc)

    qpos = i * tq + lax.broadcasted_iota(I32, (tq, tk), 0)
    kpos = j * tk + lax.broadcasted_iota(I32, (tq, tk), 1)
    nb = sel_ref.shape[1]
    shift = int(math.log2(SEL_BLOCK))
    key_blk = lax.shift_right_logical(j * tk + lax.broadcasted_iota(I32, (nb, tk), 1), shift)
    expand = jnp.where(lax.broadcasted_iota(I32, (nb, tk), 0) == key_blk, 1.0, 0.0).astype(BF16)
    picked = _dot(sel_ref[...], expand)
    valid = (picked > 0.5) & (kpos <= qpos)
    ksv = ksv_ref[...]
    for h in range(C_HEADS):
        _flash_update(q_ref[h], ksv, valid, ms_sc, ls_sc, as_sc, h)

    @pl.when(wj_ref[t] == j)
    def _():
        wvalid = (kpos <= qpos) & (kpos >= qpos - WINDOW)
        kwv = kwv_ref[...]
        for h in range(C_HEADS):
            _flash_update(q_ref[h], kwv, wvalid, mw_sc, lw_sc, aw_sc, h)

    @pl.when(lf_ref[t] == 1)
    def _():
        g = dtg_ref[...]
        for h in range(C_HEADS):
            gate = lambda br: g[:, N_DT + br * C_HEADS + h:N_DT + br * C_HEADS + h + 1]
            o_ref[h] = gate(0) * ocmp_ref[h] + gate(1) * (as_sc[h] / ls_sc[h]) + gate(2) * (aw_sc[h] / lw_sc[h])


def _nsa_attn(qw, ksv, kwv, sel, ocmp, dtg, tq, tk):
    s = qw.shape[1]
    nb = sel.shape[1]
    assert tk % tq == 0 and s % tk == 0 and WINDOW % tk == 0
    qi, kj, lf, wj = _causal_pairs(s // tq, tq, tk, WINDOW)
    hq = lambda t, qi, kj, lf, wj: (0, qi[t], 0)
    grid_spec = pltpu.PrefetchScalarGridSpec(
        num_scalar_prefetch=4,
        grid=(qi.shape[0],),
        in_specs=[
            pl.BlockSpec((C_HEADS, tq, LANES), hq),
            pl.BlockSpec((tk, LANES), lambda t, qi, kj, lf, wj: (kj[t], 0)),
            pl.BlockSpec((tk, LANES), lambda t, qi, kj, lf, wj: (wj[t], 0)),
            pl.BlockSpec((tq, nb), lambda t, qi, kj, lf, wj: (qi[t], 0)),
            pl.BlockSpec((C_HEADS, tq, LANES), hq),
            pl.BlockSpec((tq, LANES), lambda t, qi, kj, lf, wj: (qi[t], 0)),
        ],
        out_specs=pl.BlockSpec((C_HEADS, tq, LANES), hq),
        scratch_shapes=[pltpu.VMEM((C_HEADS, tq, 1), F32), pltpu.VMEM((C_HEADS, tq, 1), F32),
                        pltpu.VMEM((C_HEADS, tq, LANES), F32)] * 2,
    )
    o = pl.pallas_call(
        functools.partial(_nsa_attn_kernel, tq=tq, tk=tk),
        grid_spec=grid_spec,
        out_shape=jax.ShapeDtypeStruct((C_HEADS, s, LANES), F32),
        compiler_params=_cparams(("arbitrary",), 48),
        name="nsa_attn",
    )(qi, kj, lf, wj, qw, ksv, kwv, sel, ocmp, dtg)
    return o[:, :, C_HD:].transpose(1, 0, 2).reshape(s, C_HEADS * C_HD)


def _head_rows_padded(cq, rows):
    m = cq.shape[0]
    q = cq.reshape(m, C_HEADS, C_HD).transpose(1, 0, 2)
    return jnp.pad(q, ((0, rows - C_HEADS), (0, 0), (0, LANES - C_HD)))


def _nsa_prompt(cq, nsab, winb, dtg, cmp_p, g64):
    s = cq.shape[0]
    qw = _head_rows_padded(cq, C_HEADS)
    kcv = _compress(nsab[:, 0:LANES].reshape(s // CMP_BLOCK, CMP_K), cmp_p, g64, tb=s // CMP_BLOCK)
    ocmp, sel = _nsa_select(qw, kcv, tq=128)
    return _nsa_attn(qw, nsab[:, LANES:], winb, sel, ocmp, dtg, tq=256, tk=512)


SEQ_PER_STEP = 8
FLAG_LANE = 16


def _nsa_dec1_kernel(pt_ref, q_ref, ptv_ref, kcvp_ref, ocmp_ref, phys_ref, buf, sem, imp_sc, *, n_pages):
    g = pl.program_id(0)

    def row_copy(sb, pg):
        page = pt_ref[g * SEQ_PER_STEP + sb, pg]
        return pltpu.make_async_copy(kcvp_ref.at[pl.ds(page, 1), :], buf.at[sb, pl.ds(pg, 1), :], sem.at[0])

    def start_all(pg, carry):
        for sb in range(SEQ_PER_STEP):
            row_copy(sb, pg).start()
        return carry

    def wait_all(pg, carry):
        for sb in range(SEQ_PER_STEP):
            row_copy(sb, pg).wait()
        return carry

    lax.fori_loop(0, n_pages, start_all, 0)
    lax.fori_loop(0, n_pages, wait_all, 0)

    row8 = lax.broadcasted_iota(I32, (8, 2 * n_pages), 0)
    for sb in range(SEQ_PER_STEP):
        kcv = buf[sb].astype(BF16)
        s01 = _dot_nt(q_ref[sb].astype(BF16), kcv)
        s = jnp.concatenate([s01[0:8], s01[8:16]], axis=1)
        e = jnp.exp(s - jnp.max(s, axis=-1, keepdims=True))
        p = e / jnp.sum(e, axis=-1, keepdims=True)
        r0 = _dot(p[:, 0:n_pages].astype(BF16), kcv)
        r1 = _dot(p[:, n_pages:].astype(BF16), kcv)
        ocmp_ref[sb] = r0[:, 0:LANES] + r1[:, LANES:]
        imp_sc[sb:sb + 1, :] = jnp.sum(jnp.where(row8 < C_HEADS, p, 0.0), axis=0, keepdims=True)

    col = lax.broadcasted_iota(I32, (SEQ_PER_STEP, 2 * n_pages), 1)
    second = col >= n_pages
    ids = (2 * jnp.where(second, col - n_pages, col) + jnp.where(second, 1, 0)).astype(F32)
    ptv = ptv_ref[...].astype(F32)
    phys_all = jnp.concatenate([2.0 * ptv, 2.0 * ptv + 1.0], axis=1)
    lane = lax.broadcasted_iota(I32, (SEQ_PER_STEP, LANES), 1)

    def on_pick(it, pick, acc):
        ph = jnp.sum(jnp.where(pick, phys_all, 0.0), axis=-1, keepdims=True)
        ok = jnp.sum(jnp.where(pick, 1.0, 0.0), axis=-1, keepdims=True)
        return jnp.where(lane == it, ph, jnp.where(lane == FLAG_LANE + it, ok, acc))

    body = _pick_top(None, ids, N_SEL - 1, on_pick)
    _, acc = lax.fori_loop(0, N_SEL - 1, lambda it, c: body(it, c), (imp_sc[...], jnp.zeros((SEQ_PER_STEP, LANES), F32)))
    phys_ref[...] = acc.astype(I32)


def _nsa_dec1(q01, page_table, kcvp):
    b, n_pages = page_table.shape
    assert b % SEQ_PER_STEP == 0
    grid_spec = pltpu.PrefetchScalarGridSpec(
        num_scalar_prefetch=1,
        grid=(b // SEQ_PER_STEP,),
        in_specs=[pl.BlockSpec((SEQ_PER_STEP, 16, 256), lambda g, pt: (g, 0, 0)),
                  pl.BlockSpec((SEQ_PER_STEP, n_pages), lambda g, pt: (g, 0)),
                  pl.BlockSpec(memory_space=pl.ANY)],
        out_specs=[pl.BlockSpec((SEQ_PER_STEP, 8, LANES), lambda g, pt: (g, 0, 0)),
                   pl.BlockSpec((SEQ_PER_STEP, LANES), lambda g, pt: (g, 0))],
        scratch_shapes=[pltpu.VMEM((SEQ_PER_STEP, n_pages, 256), F32), pltpu.SemaphoreType.DMA((1,)),
                        pltpu.VMEM((SEQ_PER_STEP, 2 * n_pages), F32)],
    )
    return pl.pallas_call(
        functools.partial(_nsa_dec1_kernel, n_pages=n_pages),
        grid_spec=grid_spec,
        out_shape=[jax.ShapeDtypeStruct((b, 8, LANES), F32), jax.ShapeDtypeStruct((b, LANES), I32)],
        compiler_params=_cparams(("arbitrary",), 32),
        name="nsa_decode_select",
    )(page_table, q01, page_table, kcvp)


def _nsa_dec2_kernel(ph_ref, qw_ref, *rest, n_blk):
    blk_refs = rest[:n_blk]
    ws_ref, sn_ref, wn_ref, oc_ref, gm_ref, o_ref = rest[n_blk:]
    b = pl.program_id(0)
    qw = qw_ref[...]
    qb = qw.astype(BF16)

    def attend(kv_bf, valid, new_row):
        s = _dot_nt(qb, kv_bf)
        s_new = jnp.sum(qw * new_row, axis=-1, keepdims=True)
        if valid is not None:
            s = jnp.where(valid, s, NEG)
        m = jnp.maximum(jnp.max(s, axis=-1, keepdims=True), s_new)
        p = jnp.exp(s - m)
        if valid is not None:
            p = jnp.where(valid, p, 0.0)
        pn = jnp.exp(s_new - m)
        den = jnp.sum(p, axis=-1, keepdims=True) + pn
        return (_dot(p.astype(BF16), kv_bf) + pn * new_row) / den

    kall = jnp.concatenate([r[...] for r in blk_refs], axis=0)
    col_blk = lax.shift_right_logical(lax.broadcasted_iota(I32, (8, n_blk * SEL_BLOCK), 1), int(math.log2(SEL_BLOCK)))
    vf = jnp.zeros((8, n_blk * SEL_BLOCK), F32)
    for j in range(n_blk):
        vf = jnp.where(col_blk == j, ph_ref[b, FLAG_LANE + j].astype(F32), vf)
    o_sel = attend(kall, vf > 0.5, sn_ref[...])
    o_win = attend(ws_ref[...].astype(BF16), None, wn_ref[...])
    gm = gm_ref[...]
    o_ref[...] = gm[:, 0:1] * oc_ref[...] + gm[:, 1:2] * o_sel + gm[:, 2:3] * o_win


def _nsa_dec2(phys, qw, nselb, win_state, selnew, winnew, ocmp, gm):
    b = qw.shape[0]
    n_blk = N_SEL - 1
    wlen = win_state.shape[1]
    per_b = lambda shape: pl.BlockSpec((None,) + shape, lambda bi, ph: (bi, 0, 0))
    blk_specs = [pl.BlockSpec((None, SEL_BLOCK, LANES), functools.partial(lambda bi, ph, j: (ph[bi, j], 0, 0), j=j))
                 for j in range(n_blk)]
    grid_spec = pltpu.PrefetchScalarGridSpec(
        num_scalar_prefetch=1,
        grid=(b,),
        in_specs=[per_b((8, LANES))] + blk_specs + [per_b((wlen, LANES)), per_b((1, LANES)), per_b((1, LANES)),
                                                   per_b((8, LANES)), per_b((8, LANES))],
        out_specs=per_b((8, LANES)),
    )
    o = pl.pallas_call(
        functools.partial(_nsa_dec2_kernel, n_blk=n_blk),
        grid_spec=grid_spec,
        out_shape=jax.ShapeDtypeStruct((b, 8, LANES), F32),
        compiler_params=_cparams(("arbitrary",), 32),
        name="nsa_decode_attend",
    )(phys, qw, *([nselb] * n_blk), win_state, selnew, winnew, ocmp, gm)
    return o[:, 0:C_HEADS, C_HD:].reshape(b, C_HEADS * C_HD)


def _nsa_decode(cq, nsa_new, win_new, dtg, pool_n, win_state, page_table, cmp_p, g64):
    b = cq.shape[0]
    n_pool, page = pool_n.shape[0], pool_n.shape[1]
    assert page == 2 * CMP_BLOCK and CMP_BLOCK == SEL_BLOCK
    cmp_rows = pool_n[:, :, 0:2, :].reshape(n_pool * 2, CMP_K).astype(BF16)
    tb = 512 if (n_pool * 2) % 512 == 0 else n_pool * 2
    kcvp = _compress(cmp_rows, cmp_p, g64, tb=tb).reshape(n_pool, 2 * LANES)
    nselb = pool_n[:, :, 2:4, :].reshape(n_pool * 2, SEL_BLOCK, LANES).astype(BF16)
    q4 = cq.astype(F32).reshape(b, C_HEADS, C_HD)
    q_first = jnp.pad(q4, ((0, 0), (0, 8 - C_HEADS), (0, 2 * LANES - C_HD)))
    q_second = jnp.pad(q4, ((0, 0), (0, 8 - C_HEADS), (LANES, LANES - C_HD)))
    ocmp, phys = _nsa_dec1(jnp.concatenate([q_first, q_second], axis=1), page_table, kcvp)
    qw = jnp.pad(q4, ((0, 0), (0, 8 - C_HEADS), (0, LANES - C_HD)))
    gates = dtg[:, N_DT:N_DT + N_GATE].reshape(b, 3, C_HEADS).transpose(0, 2, 1)
    gm = jnp.pad(gates, ((0, 0), (0, 8 - C_HEADS), (0, LANES - 3)))
    ws = win_state.reshape(b, win_state.shape[1], LANES)
    o = _nsa_dec2(phys, qw, nselb, ws, nsa_new[:, None, LANES:], win_new[:, None, :], ocmp, gm)
    new_win = jnp.concatenate([ws[:, 1:], win_new[:, None, :]], axis=1)
    return o, new_win.reshape(win_state.shape)


def kernel(x_prompt, x_sample, cache_diff_kv, cache_nsa_kv, state_nsa_win, state_ssm, state_conv, page_table, ffn1_norm, ffn1_w_gate, ffn1_w_up, ffn1_w_down, mix_norm, w_in, diff_q_norm, diff_k_norm, diff_lambda, diff_subln, conv_w, conv_b, dt_bias, a_log, d_skip, ssm_norm, nsa_q_norm, nsa_k_norm, nsa_cmp_pe, nsa_cmp_w, w_out, ffn2_norm, ffn2_w_gate, ffn2_w_up, ffn2_w_down):
    P = dict(ffn1_norm=ffn1_norm, ffn1_w_gate=ffn1_w_gate, ffn1_w_up=ffn1_w_up, ffn1_w_down=ffn1_w_down, mix_norm=mix_norm,
             w_in=w_in, diff_q_norm=diff_q_norm, diff_k_norm=diff_k_norm, conv_w=conv_w, conv_b=conv_b, dt_bias=dt_bias,
             a_log=a_log, d_skip=d_skip, ssm_norm=ssm_norm, nsa_q_norm=nsa_q_norm, nsa_k_norm=nsa_k_norm,
             nsa_cmp_pe=nsa_cmp_pe, nsa_cmp_w=nsa_cmp_w, w_out=w_out, ffn2_norm=ffn2_norm, ffn2_w_gate=ffn2_w_gate,
             ffn2_w_up=ffn2_w_up, ffn2_w_down=ffn2_w_down)
    depth = w_in.shape[0]
    xp = x_prompt[0]
    xs = x_sample[:, 0]
    s, b = xp.shape[0], xs.shape[0]
    n_pool, page = cache_diff_kv.shape[1], cache_diff_kv.shape[2]
    tm = min(ROW_TILE, s)
    wlen = min(WINDOW, s)
    outs = [[] for _ in range(10)]
    for l in range(depth):
        lam_init = 0.8 - 0.6 * math.exp(-0.3 * l)
        p = _prep_layer(l, P)
        sp = _ssd_params(P, l)
        cmp_p = _cmp_params(P, l)
        dl = diff_lambda[l]
        subg = diff_subln[l]
        xp = _ffn(xp, *p["ffn1"], tm=tm)
        xs = _ffn(xs, *p["ffn1"], tm=b)
        qd, dkv, dkvb, z, xbc, dtg, cq, nsa, nsab, win, winb = _inproj(xp, p, tm=tm)
        od = _diff_prompt(qd, dkvb, dl, subg[None, :], lam_init, tq=ATTN_TQ, tk=ATTN_TK)
        osm, ssm_f = _ssd_prompt(z, xbc, dtg, sp)
        on = _nsa_prompt(cq, nsab, winb, dtg, cmp_p, p["g64"])
        xp = _mixout(xp, od, osm, on, p["w_out"], tm=tm)
        prompt_outs = (dkv.reshape(1, s, 2, A_HEADS, A_HD), nsa.reshape(1, s, 4, C_HD),
                       win[s - wlen:].reshape(1, wlen, 2, C_HD), ssm_f[None], xbc[s - (CONV_W - 1):][None])
        sqd, sdkv, _, sz, sxbc, sdtg, scq, snsa, _, swin, _ = _inproj(xs, p, tm=b)
        pool_bf = cache_diff_kv[l].reshape(n_pool, page, 2 * A_HEADS * A_HD).astype(BF16)
        sod = _diff_decode(sqd, sdkv, pool_bf, page_table, dl, jnp.tile(subg, A_HEADS)[None, :], lam_init,
                           n_pg=DECODE_PAGES_PER_STEP)
        sos, sst, scs = _ssd_decode(sz, sxbc, sdtg, state_conv[l], state_ssm[l], sp)
        son, new_win = _nsa_decode(scq, snsa, swin, sdtg, cache_nsa_kv[l], state_nsa_win[l], page_table, cmp_p, p["g64"])
        xs = _mixout(xs, sod, sos, son, p["w_out"], tm=b)
        sample_outs = (sdkv.reshape(b, 1, 2, A_HEADS, A_HD), snsa.reshape(b, 1, 4, C_HD), new_win, sst, scs)
        for acc, o in zip(outs, prompt_outs + sample_outs):
            acc.append(o)
        xp = _ffn(xp, *p["ffn2"], tm=tm)
        xs = _ffn(xs, *p["ffn2"], tm=b)
    return (xp[None], xs[:, None]) + tuple(jnp.stack(o) for o in outs)
```

```python
import functools
import math

import jax
import jax.numpy as jnp
import numpy as np
from jax import lax
from jax.experimental import pallas as pl
from jax.experimental.pallas import tpu as pltpu

F32, BF16, I32 = jnp.float32, jnp.bfloat16, jnp.int32

A_HEADS, A_QK, A_HD = 4, 32, 64
M_HEADS, M_HD, M_INNER, M_GROUPS, M_STATE, CONV_W, M_CONV_DIM, M_CHUNK = 8, 64, 512, 2, 128, 4, 1024, 128
C_HEADS, C_HD, CMP_BLOCK, SEL_BLOCK, N_SEL, WINDOW = 4, 64, 64, 64, 16, 512
EPS = 1e-6
NEG = -1e30
LOG2E = math.log2(math.e)
LANES = 128
OFF_AQ, OFF_AK, OFF_AV, OFF_Z, OFF_XBC, OFF_CQ, OFF_CKV, OFF_DTG, W_IN_COLS = 0, 256, 512, 768, 1280, 2304, 2560, 2944, 3072
N_DT = M_HEADS
N_GATE = 3 * C_HEADS
ROW_TILE = 512
ATTN_TQ, ATTN_TK = 256, 512
DECODE_PAGES_PER_STEP = 8


def _cparams(sem, vmem_mib):
    return pltpu.CompilerParams(dimension_semantics=sem, vmem_limit_bytes=vmem_mib * 1024 * 1024)


def _dot(a, b):
    return jnp.dot(a, b, preferred_element_type=F32)


def _dot_nt(a, b):
    return lax.dot_general(a, b, (((1,), (1,)), ((), ())), preferred_element_type=F32)


def _split3(x):
    x1 = x.astype(BF16)
    r = x - x1.astype(F32)
    x2 = r.astype(BF16)
    x3 = (r - x2.astype(F32)).astype(BF16)
    return x1, x2, x3


def _dot_exact_rhs(x, a_bf):
    x1, x2, x3 = _split3(x)
    return _dot(x1, a_bf) + _dot(x2, a_bf) + _dot(x3, a_bf)


def _dot_exact_lhs(a_bf, x):
    x1, x2, x3 = _split3(x)
    return _dot(a_bf, x1) + _dot(a_bf, x2) + _dot(a_bf, x3)


def _sigmoid(x):
    return 1.0 / (1.0 + jnp.exp(-x))


def _silu(x):
    return x * _sigmoid(x)


def _softplus(x):
    return jnp.maximum(x, 0.0) + jnp.log1p(jnp.exp(-jnp.abs(x)))


def _rms_rows(x, g):
    return x * lax.rsqrt(jnp.mean(x * x, axis=-1, keepdims=True) + EPS) * g


def _group_norm(v, gmat_bf):
    sq = v * v
    hi = sq.astype(BF16)
    lo = (sq - hi.astype(F32)).astype(BF16)
    ms = _dot(hi, gmat_bf) + _dot(lo, gmat_bf)
    return v * lax.rsqrt(ms + EPS)


def _ffn_kernel(x_ref, g_ref, wg_ref, wu_ref, wd_ref, o_ref, h_sc, acc_sc):
    c = pl.program_id(1)

    @pl.when(c == 0)
    def _():
        h_sc[...] = _rms_rows(x_ref[...], g_ref[...]).astype(BF16)
        acc_sc[...] = jnp.zeros_like(acc_sc)

    h = h_sc[...]
    a = _dot(h, wg_ref[...])
    b = _dot(h, wu_ref[...])
    acc_sc[...] += _dot((_silu(a) * b).astype(BF16), wd_ref[...])

    @pl.when(c == pl.num_programs(1) - 1)
    def _():
        o_ref[...] = x_ref[...] + 0.5 * acc_sc[...]


def _ffn(x, g, wg, wu, wd, tm):
    m, d = x.shape
    ff = wg.shape[1]
    fc = ff // 2
    return pl.pallas_call(
        _ffn_kernel,
        grid=(m // tm, ff // fc),
        in_specs=[
            pl.BlockSpec((tm, d), lambda i, c: (i, 0)),
            pl.BlockSpec((1, d), lambda i, c: (0, 0)),
            pl.BlockSpec((d, fc), lambda i, c: (0, c)),
            pl.BlockSpec((d, fc), lambda i, c: (0, c)),
            pl.BlockSpec((fc, d), lambda i, c: (c, 0)),
        ],
        out_specs=pl.BlockSpec((tm, d), lambda i, c: (i, 0)),
        out_shape=jax.ShapeDtypeStruct((m, d), F32),
        scratch_shapes=[pltpu.VMEM((tm, d), BF16), pltpu.VMEM((tm, d), F32)],
        compiler_params=_cparams(("parallel", "arbitrary"), 56),
        name="ffn",
    )(x, g, wg, wu, wd)


def _inproj_kernel(x_ref, g_ref, w_ref, g32_ref, g64_ref, qg_ref, kg_ref, cqg_ref, ng_ref, nm_ref, wg_ref, wm_ref,
                   qd_ref, dkv_ref, dkvb_ref, z_ref, xbc_ref, dtg_ref, cq_ref, nsa_ref, nsab_ref,
                   win_ref, winb_ref):
    h = _rms_rows(x_ref[...], g_ref[...]).astype(BF16)

    def seg(a, b):
        return _dot(h, w_ref[:, a:b])

    g32 = g32_ref[...]
    g64 = g64_ref[...]
    qd_ref[...] = (_group_norm(seg(OFF_AQ, OFF_AK), g32) * qg_ref[...]).astype(BF16)
    kn = _group_norm(seg(OFF_AK, OFF_AV), g32) * kg_ref[...]
    av = seg(OFF_AV, OFF_Z)
    dkv_ref[:, 0:256] = kn
    dkv_ref[:, 256:512] = av
    dkvb_ref[:, 0:256] = kn.astype(BF16)
    dkvb_ref[:, 256:512] = av.astype(BF16)
    z_ref[...] = seg(OFF_Z, OFF_XBC)
    xbc_ref[...] = seg(OFF_XBC, OFF_CQ)
    cq_ref[...] = (_group_norm(seg(OFF_CQ, OFF_CKV), g64) * cqg_ref[...]).astype(BF16)
    nr = seg(OFF_CKV, OFF_CKV + 256)
    nsa = jnp.where(nm_ref[...] > 0.5, _group_norm(nr, g64) * ng_ref[...], nr)
    nsa_ref[...] = nsa
    nsab_ref[...] = nsa.astype(BF16)
    wr = seg(OFF_CKV + 256, OFF_DTG)
    win = jnp.where(wm_ref[...] > 0.5, _group_norm(wr, g64_ref[0:128, 0:128]) * wg_ref[...], wr)
    win_ref[...] = win
    winb_ref[...] = win.astype(BF16)
    d = seg(OFF_DTG, W_IN_COLS)
    lane = lax.broadcasted_iota(I32, d.shape, 1)
    dtg_ref[...] = jnp.where(lane >= N_DT, _sigmoid(d), d)


def _inproj(x, p, tm):
    m, d = x.shape
    row = lambda w: pl.BlockSpec((tm, w), lambda i: (i, 0))
    full = lambda a: pl.BlockSpec(a.shape, lambda i: (0,) * a.ndim)
    consts = [p["mix_norm"], p["w_in"], p["g32"], p["g64"], p["qg"], p["kg"], p["cqg"], p["ng"], p["nm"], p["wg"], p["wm"]]
    outs = [(256, BF16), (512, F32), (512, BF16), (512, F32), (1024, F32), (128, F32), (256, BF16),
            (256, F32), (256, BF16), (128, F32), (128, BF16)]
    return pl.pallas_call(
        _inproj_kernel,
        grid=(m // tm,),
        in_specs=[row(d)] + [full(a) for a in consts],
        out_specs=[row(w) for w, _ in outs],
        out_shape=[jax.ShapeDtypeStruct((m, w), dt) for w, dt in outs],
        compiler_params=_cparams(("parallel",), 56),
        name="inproj",
    )(x, *consts)


def _mixout_kernel(x_ref, od_ref, os_ref, on_ref, w_ref, o_ref):
    o_ref[...] = (x_ref[...]
                  + _dot(od_ref[...].astype(BF16), w_ref[0:256, :])
                  + _dot(os_ref[...].astype(BF16), w_ref[256:768, :])
                  + _dot(on_ref[...].astype(BF16), w_ref[768:1024, :]))


def _mixout(x, od, osm, on, w, tm):
    m, d = x.shape
    row = lambda w_: pl.BlockSpec((tm, w_), lambda i: (i, 0))
    return pl.pallas_call(
        _mixout_kernel,
        grid=(m // tm,),
        in_specs=[row(d), row(256), row(512), row(256), pl.BlockSpec(w.shape, lambda i: (0, 0))],
        out_specs=row(d),
        out_shape=jax.ShapeDtypeStruct((m, d), F32),
        compiler_params=_cparams(("parallel",), 40),
        name="mixout",
    )(x, od, osm, on, w)


IN_SPLITS = (256, 256, 256, 512, 1024, 8, 256, 384, 12)


def _block_diag_mean(width, group):
    m = np.zeros((width, width), np.float32)
    for s in range(0, width, group):
        m[s:s + group, s:s + group] = 1.0 / group
    return jnp.asarray(m, BF16)


def _prep_layer(l, P):
    d = P["w_in"].shape[1]
    offs = np.cumsum((0,) + IN_SPLITS)
    cols = [P["w_in"][l][:, offs[i]:offs[i + 1]] for i in range(len(IN_SPLITS))]
    aq, ak, av, z, xbc, dt, cq, ckv, cg = cols
    cg = cg.reshape(d, C_HEADS, 3).transpose(0, 2, 1).reshape(d, N_GATE)
    dtg = jnp.concatenate([dt, cg, jnp.zeros((d, LANES - N_DT - N_GATE), F32)], axis=1)
    w_in = jnp.concatenate([aq, ak, av, z, xbc, cq, ckv, dtg], axis=1).astype(BF16)
    ones64 = jnp.ones((C_HD,), F32)
    kn = P["nsa_k_norm"][l]
    p = {
        "mix_norm": P["mix_norm"][l][None, :],
        "w_in": w_in,
        "g32": _block_diag_mean(256, A_QK),
        "g64": _block_diag_mean(256, C_HD),
        "qg": (jnp.tile(P["diff_q_norm"][l], 2 * A_HEADS) * (A_QK ** -0.5 * LOG2E))[None, :],
        "kg": jnp.tile(P["diff_k_norm"][l], 2 * A_HEADS)[None, :],
        "cqg": (jnp.tile(P["nsa_q_norm"][l], C_HEADS) * (C_HD ** -0.5 * LOG2E))[None, :],
        "ng": jnp.concatenate([ones64, ones64, kn[1], ones64])[None, :],
        "nm": jnp.concatenate([0 * ones64, 0 * ones64, ones64, 0 * ones64])[None, :],
        "wg": jnp.concatenate([kn[2], ones64])[None, :],
        "wm": jnp.concatenate([ones64, 0 * ones64])[None, :],
    }
    for f in ("ffn1", "ffn2"):
        p[f] = (P[f + "_norm"][l][None, :], P[f + "_w_gate"][l].astype(BF16), P[f + "_w_up"][l].astype(BF16),
                P[f + "_w_down"][l].astype(BF16))
    p["w_out"] = P["w_out"][l].astype(BF16)
    return p


def _causal_pairs(n_q, tq, tk, window=None):
    qi, kj, lf, wj = [], [], [], []
    for i in range(n_q):
        last = (i * tq + tq - 1) // tk
        first_w = max(0, (i * tq - (window or 0)) // tk)
        for j in range(last + 1):
            qi.append(i), kj.append(j), lf.append(int(j == last)), wj.append(max(j, first_w))
    return tuple(jnp.asarray(np.asarray(a, np.int32)) for a in (qi, kj, lf, wj))


def _diff_lambda(dl, lam_init):
    a = jnp.sum(dl[0:1] * dl[1:2], axis=-1, keepdims=True)
    b = jnp.sum(dl[2:3] * dl[3:4], axis=-1, keepdims=True)
    return jnp.exp(a) - jnp.exp(b) + lam_init


def _diff_prompt_kernel(qi_ref, kj_ref, lf_ref, q_ref, k_ref, vT_ref, dl_ref, sg_ref, o_ref,
                        qm_sc, m_sc, l_sc, acc_sc, *, tq, tk, lam_init):
    t = pl.program_id(0)
    i = qi_ref[t]
    j = kj_ref[t]
    n_hc = 2 * A_HEADS

    @pl.when(j == 0)
    def _():
        q = q_ref[...]
        lane = lax.broadcasted_iota(I32, q.shape, 1)
        for hc in range(n_hc):
            qm_sc[hc] = jnp.where((lane >= A_QK * hc) & (lane < A_QK * (hc + 1)), q, jnp.zeros_like(q))
        m_sc[...] = jnp.full(m_sc.shape, NEG, F32)
        l_sc[...] = jnp.zeros_like(l_sc)
        acc_sc[...] = jnp.zeros_like(acc_sc)

    def sweep(masked):
        k = k_ref[...]
        if masked:
            kpos = j * tk + lax.broadcasted_iota(I32, (tk, tq), 0)
            qpos = i * tq + lax.broadcasted_iota(I32, (tk, tq), 1)
            keep = kpos <= qpos
        for hc in range(n_hc):
            s = _dot_nt(k, qm_sc[hc])
            if masked:
                s = jnp.where(keep, s, NEG)
            m_prev = m_sc[hc]
            m_new = jnp.maximum(m_prev, jnp.max(s, axis=0, keepdims=True))
            alpha = jnp.exp2(m_prev - m_new)
            p = jnp.exp2(s - m_new)
            l_sc[hc] = alpha * l_sc[hc] + jnp.sum(p, axis=0, keepdims=True)
            acc_sc[hc] = alpha * acc_sc[hc] + _dot(vT_ref[hc // 2], p.astype(BF16))
            m_sc[hc] = m_new

    diag = (j * tk + tk - 1) > (i * tq)
    pl.when(diag)(lambda: sweep(True))
    pl.when(jnp.logical_not(diag))(lambda: sweep(False))

    @pl.when(lf_ref[t] == 1)
    def _():
        lam = _diff_lambda(dl_ref[...], lam_init)
        for h in range(A_HEADS):
            o = acc_sc[2 * h] / l_sc[2 * h] - lam * (acc_sc[2 * h + 1] / l_sc[2 * h + 1])
            ms = jnp.mean(o * o, axis=0, keepdims=True)
            o_ref[h] = o * lax.rsqrt(ms + EPS) * sg_ref[...] * (1.0 - lam_init)


def _diff_prompt(qd, dkvb, dl, subg, lam_init, tq, tk):
    s = qd.shape[0]
    assert tk % tq == 0 and s % tk == 0
    vT = dkvb[:, 256:].reshape(s, A_HEADS, A_HD).transpose(1, 2, 0)
    qi, kj, lf, _ = _causal_pairs(s // tq, tq, tk)
    grid_spec = pltpu.PrefetchScalarGridSpec(
        num_scalar_prefetch=3,
        grid=(qi.shape[0],),
        in_specs=[
            pl.BlockSpec((tq, 256), lambda t, qi, kj, lf: (qi[t], 0)),
            pl.BlockSpec((tk, 256), lambda t, qi, kj, lf: (kj[t], 0)),
            pl.BlockSpec((A_HEADS, A_HD, tk), lambda t, qi, kj, lf: (0, 0, kj[t])),
            pl.BlockSpec((4, A_QK), lambda t, qi, kj, lf: (0, 0)),
            pl.BlockSpec((A_HD, 1), lambda t, qi, kj, lf: (0, 0)),
        ],
        out_specs=pl.BlockSpec((A_HEADS, A_HD, tq), lambda t, qi, kj, lf: (0, 0, qi[t])),
        scratch_shapes=[pltpu.VMEM((2 * A_HEADS, tq, 256), BF16), pltpu.VMEM((2 * A_HEADS, 1, tq), F32),
                        pltpu.VMEM((2 * A_HEADS, 1, tq), F32), pltpu.VMEM((2 * A_HEADS, A_HD, tq), F32)],
    )
    o = pl.pallas_call(
        functools.partial(_diff_prompt_kernel, tq=tq, tk=tk, lam_init=lam_init),
        grid_spec=grid_spec,
        out_shape=jax.ShapeDtypeStruct((A_HEADS, A_HD, s), F32),
        compiler_params=_cparams(("arbitrary",), 48),
        name="diff_prompt",
    )(qi, kj, lf, qd, dkvb, vT, dl, subg.reshape(A_HD, 1))
    return o.transpose(2, 0, 1).reshape(s, A_HEADS * A_HD)


def _diff_decode_kernel(pt_ref, q_ref, new_ref, *rest, n_pg, lam_init):
    page_refs = rest[:n_pg]
    dl_ref, sg_ref, o_ref, m_sc, l_sc, acc_sc = rest[n_pg:]
    step = pl.program_id(1)
    n_hc = 2 * A_HEADS
    lane = lax.broadcasted_iota(I32, (n_hc, 256), 1)
    row = lax.broadcasted_iota(I32, (n_hc, 256), 0)
    qm = jnp.where((lane >= A_QK * row) & (lane < A_QK * (row + 1)), q_ref[0], 0.0)

    @pl.when(step == 0)
    def _():
        new = new_ref[0]
        m_sc[...] = jnp.sum(qm * new[:, 0:256], axis=-1, keepdims=True)
        l_sc[...] = jnp.ones_like(l_sc)
        acc_sc[...] = jnp.broadcast_to(new[:, 256:512], acc_sc.shape)

    qb = qm.astype(BF16)
    n_kd = 2 * A_HEADS * A_QK
    s = jnp.concatenate([_dot(qb, r[0:n_kd, :].astype(BF16)) for r in page_refs], axis=1)
    m_prev = m_sc[...]
    m_new = jnp.maximum(m_prev, jnp.max(s, axis=-1, keepdims=True))
    alpha = jnp.exp2(m_prev - m_new)
    p = jnp.exp2(s - m_new)
    l_sc[...] = alpha * l_sc[...] + jnp.sum(p, axis=-1, keepdims=True)
    p = p.astype(BF16)
    page = page_refs[0].shape[1]
    pv = _dot_nt(p[:, 0:page], page_refs[0][n_kd:, :].astype(BF16))
    for g in range(1, n_pg):
        pv = pv + _dot_nt(p[:, g * page:(g + 1) * page], page_refs[g][n_kd:, :].astype(BF16))
    acc_sc[...] = alpha * acc_sc[...] + pv
    m_sc[...] = m_new

    @pl.when(step == pl.num_programs(1) - 1)
    def _():
        lam = _diff_lambda(dl_ref[...], lam_init)
        a = acc_sc[...] / l_sc[...]
        lane1 = lax.broadcasted_iota(I32, (1, 256), 1)
        o = jnp.zeros((1, 256), F32)
        for h in range(A_HEADS):
            in_h = (lane1 >= A_HD * h) & (lane1 < A_HD * (h + 1))
            oh = a[2 * h:2 * h + 1, :] - lam * a[2 * h + 1:2 * h + 2, :]
            ms = jnp.sum(jnp.where(in_h, oh * oh, 0.0), axis=-1, keepdims=True) * (1.0 / A_HD)
            o = jnp.where(in_h, oh * lax.rsqrt(ms + EPS), o)
        o_ref[0] = o * sg_ref[...] * (1.0 - lam_init)


def _diff_decode(qd, dkv_new, pool_t, l, page_table, dl, subg4, lam_init, n_pg):
    b, n_pages = page_table.shape
    page = pool_t.shape[3]
    assert n_pages % n_pg == 0
    cmap = lambda bi, si, pt: (0, 0)
    page_specs = [pl.BlockSpec((None, None, 512, page),
                               functools.partial(lambda bi, si, pt, g: (l, pt[bi, si * n_pg + g], 0, 0), g=g))
                  for g in range(n_pg)]
    grid_spec = pltpu.PrefetchScalarGridSpec(
        num_scalar_prefetch=1,
        grid=(b, n_pages // n_pg),
        in_specs=[pl.BlockSpec((1, 1, 256), lambda bi, si, pt: (bi, 0, 0)),
                  pl.BlockSpec((1, 1, 512), lambda bi, si, pt: (bi, 0, 0))] + page_specs +
                 [pl.BlockSpec((4, A_QK), cmap), pl.BlockSpec((1, 256), cmap)],
        out_specs=pl.BlockSpec((1, 1, 256), lambda bi, si, pt: (bi, 0, 0)),
        scratch_shapes=[pltpu.VMEM((2 * A_HEADS, 1), F32), pltpu.VMEM((2 * A_HEADS, 1), F32),
                        pltpu.VMEM((2 * A_HEADS, 256), F32)],
    )
    o = pl.pallas_call(
        functools.partial(_diff_decode_kernel, n_pg=n_pg, lam_init=lam_init),
        grid_spec=grid_spec,
        out_shape=jax.ShapeDtypeStruct((b, 1, 256), F32),
        compiler_params=_cparams(("parallel", "arbitrary"), 48),
        name="diff_decode",
    )(page_table, qd.astype(F32).reshape(b, 1, 256), dkv_new.reshape(b, 1, 512), *([pool_t] * n_pg), dl, subg4)
    return o.reshape(b, 256)


def _ssd_consts():
    tri = np.tril(np.ones((M_CHUNK, M_CHUNK), np.float32))
    expand = np.zeros((LANES, M_INNER), np.float32)
    for h in range(M_HEADS):
        expand[h, h * M_HD:(h + 1) * M_HD] = 1.0
    return jnp.asarray(tri, BF16), jnp.asarray(tri.T, BF16), jnp.asarray(expand, BF16)


def _ssd_params(P, l):
    pad = lambda v: jnp.concatenate([v, jnp.zeros((LANES - M_HEADS,), F32)])[None, :]
    return dict(cw=P["conv_w"][l], cb=P["conv_b"][l][None, :], dtb=pad(P["dt_bias"][l]), dtbT=P["dt_bias"][l][:, None],
                alog=pad(P["a_log"][l]), alogT=P["a_log"][l][:, None],
                dskip=jnp.repeat(P["d_skip"][l], M_HD)[None, :], nrm=P["ssm_norm"][l][None, :])


def _gated_group_norm(y, x, z, dskip, nrm):
    g = (y + x * dskip) * _silu(z)
    gw = M_INNER // M_GROUPS
    return jnp.concatenate([_rms_rows(g[:, i * gw:(i + 1) * gw], nrm[:, i * gw:(i + 1) * gw]) for i in range(M_GROUPS)], axis=1)


def _ssd_prompt_kernel(z_ref, xbc_ref, dtg_ref, dtT_ref, cw_ref, cb_ref, dtb_ref, dtbT_ref, alog_ref, alogT_ref,
                       dskip_ref, nrm_ref, tri_ref, triT_ref, ex_ref, y_ref, hT_ref, xpad_sc, h_sc):
    c = pl.program_id(0)
    q = M_CHUNK

    @pl.when(c == 0)
    def _():
        xpad_sc[0:8, :] = jnp.zeros((8, M_CONV_DIM), F32)
        h_sc[...] = jnp.zeros_like(h_sc)

    xpad_sc[8:8 + q, :] = xbc_ref[...]
    conv = cb_ref[...]
    for j in range(CONV_W):
        conv = conv + cw_ref[j:j + 1, :] * xpad_sc[8 - (CONV_W - 1) + j:8 - (CONV_W - 1) + j + q, :]
    xpad_sc[0:8, :] = xbc_ref[q - 8:q, :]
    xa = _silu(conv)
    x = xa[:, 0:M_INNER]
    bm = xa[:, M_INNER:M_INNER + M_GROUPS * M_STATE]
    cm = xa[:, M_INNER + M_GROUPS * M_STATE:]

    lane = lax.broadcasted_iota(I32, (1, LANES), 1)
    is_head = lane < M_HEADS
    a_row = jnp.where(is_head, -jnp.exp(alog_ref[...]), 0.0)
    dt = jnp.where(is_head, _softplus(dtg_ref[...] + dtb_ref[...]), 0.0)
    acum = _dot_exact_lhs(tri_ref[...], dt * a_row)
    daT = _softplus(dtT_ref[...] + dtbT_ref[...]) * (-jnp.exp(alogT_ref[...]))
    acumT = _dot_exact_rhs(daT, triT_ref[...])
    ex = ex_ref[...]
    acum_x = _dot_exact_rhs(acum, ex)
    dt_x = _dot_exact_rhs(dt, ex)
    last = acum_x[q - 1:q, :]
    xdt = x * dt_x
    xdd = (xdt * jnp.exp(last - acum_x)).astype(BF16)
    h_prev = h_sc[...]

    rowi = lax.broadcasted_iota(I32, (q, q), 0)
    coli = lax.broadcasted_iota(I32, (q, q), 1)
    causal = rowi >= coli
    lane_q = lax.broadcasted_iota(I32, (q, LANES), 1)
    hpg = M_HEADS // M_GROUPS
    gw = hpg * M_HD
    y_diag, y_off, st = [], [], []
    for g in range(M_GROUPS):
        c_g = cm[:, g * M_STATE:(g + 1) * M_STATE].astype(BF16)
        bT_g = bm[:, g * M_STATE:(g + 1) * M_STATE].T.astype(BF16)
        y_off.append(_dot(c_g, h_prev[:, g * gw:(g + 1) * gw].astype(BF16)))
        st.append(_dot(bT_g, xdd[:, g * gw:(g + 1) * gw]))
        cb = _dot(c_g, bT_g)
        for pr in range(hpg // 2):
            ms = []
            for hh in range(2):
                h = g * hpg + pr * 2 + hh
                seg = acum[:, h:h + 1] - acumT[h:h + 1, :]
                ms.append(cb * jnp.where(causal, jnp.exp(jnp.where(causal, seg, 0.0)), 0.0))
            xp = xdt[:, (g * hpg + pr * 2) * M_HD:(g * hpg + pr * 2 + 2) * M_HD]
            rhs = jnp.concatenate([jnp.where(lane_q < M_HD, xp, 0.0), jnp.where(lane_q >= M_HD, xp, 0.0)], axis=0)
            y_diag.append(_dot(jnp.concatenate(ms, axis=1).astype(BF16), rhs.astype(BF16)))
    y = jnp.concatenate(y_diag, axis=1) + jnp.concatenate(y_off, axis=1) * jnp.exp(acum_x)
    h_new = h_prev * jnp.exp(last) + jnp.concatenate(st, axis=1)
    h_sc[...] = h_new
    y_ref[...] = _gated_group_norm(y, x, z_ref[...], dskip_ref[...], nrm_ref[...])

    @pl.when(c == pl.num_programs(0) - 1)
    def _():
        hT_ref[...] = h_new


def _ssd_prompt(z, xbc, dtg, sp):
    s = z.shape[0]
    q = M_CHUNK
    tri, triT, ex = _ssd_consts()
    dtT = dtg[:, :M_HEADS].T
    consts = [sp["cw"], sp["cb"], sp["dtb"], sp["dtbT"], sp["alog"], sp["alogT"], sp["dskip"], sp["nrm"], tri, triT, ex]
    row = lambda w: pl.BlockSpec((q, w), lambda c: (c, 0))
    y, hT = pl.pallas_call(
        _ssd_prompt_kernel,
        grid=(s // q,),
        in_specs=[row(M_INNER), row(M_CONV_DIM), row(LANES), pl.BlockSpec((M_HEADS, q), lambda c: (0, c))]
                 + [pl.BlockSpec(a.shape, lambda c: (0, 0)) for a in consts],
        out_specs=[row(M_INNER), pl.BlockSpec((M_STATE, M_INNER), lambda c: (0, 0))],
        out_shape=[jax.ShapeDtypeStruct((s, M_INNER), F32), jax.ShapeDtypeStruct((M_STATE, M_INNER), F32)],
        scratch_shapes=[pltpu.VMEM((8 + q, M_CONV_DIM), F32), pltpu.VMEM((M_STATE, M_INNER), F32)],
        compiler_params=_cparams(("arbitrary",), 40),
        name="ssd_prompt",
    )(z, xbc, dtg, dtT, *consts)
    return y, hT.reshape(M_STATE, M_HEADS, M_HD).transpose(1, 2, 0)


def _ssd_decode_kernel(z_ref, xbc_ref, dtg_ref, cs_ref, st_ref, cw_ref, cb_ref, dtb_ref, alog_ref, dskip_ref, nrm_ref,
                       ex_ref, y_ref, sto_ref, cso_ref):
    b = z_ref.shape[0]
    xbc = xbc_ref[...]
    conv = cb_ref[...] + cw_ref[CONV_W - 1:CONV_W, :] * xbc
    for j in range(CONV_W - 1):
        conv = conv + cw_ref[j:j + 1, :] * cs_ref[j]
        if j > 0:
            cso_ref[j - 1] = cs_ref[j]
    cso_ref[CONV_W - 2] = xbc
    xa = _silu(conv)
    x = xa[:, 0:M_INNER]
    bm = xa[:, M_INNER:M_INNER + M_GROUPS * M_STATE]
    cm = xa[:, M_INNER + M_GROUPS * M_STATE:]
    lane = lax.broadcasted_iota(I32, (1, LANES), 1)
    is_head = lane < M_HEADS
    dt = jnp.where(is_head, _softplus(dtg_ref[...] + dtb_ref[...]), 0.0)
    da = dt * jnp.where(is_head, -jnp.exp(alog_ref[...]), 0.0)
    ex = ex_ref[...]
    xdt = x * _dot_exact_rhs(dt, ex)
    dec = jnp.exp(_dot_exact_rhs(da, ex))
    stack = jnp.concatenate([xdt, dec, jnp.zeros((LANES - 2 * b, M_INNER), F32)], axis=0)
    cols = stack.T
    lane_c = lax.broadcasted_iota(I32, (M_INNER, LANES), 1)
    ycols = jnp.zeros((M_INNER, LANES), F32)
    gw = M_INNER // M_GROUPS
    for s in range(b):
        xcol = cols[:, s:s + 1]
        dcol = cols[:, b + s:b + s + 1]
        h = st_ref[s]
        parts = []
        for g in range(M_GROUPS):
            rows = slice(g * gw, (g + 1) * gw)
            hn = h[rows] * dcol[rows] + xcol[rows] * bm[s:s + 1, g * M_STATE:(g + 1) * M_STATE]
            sto_ref[s, rows, :] = hn
            parts.append(jnp.sum(hn * cm[s:s + 1, g * M_STATE:(g + 1) * M_STATE], axis=-1, keepdims=True))
        ycols = jnp.where(lane_c == s, jnp.concatenate(parts, axis=0), ycols)
    y = ycols.T[0:b, :]
    y_ref[...] = _gated_group_norm(y, x, z_ref[...], dskip_ref[...], nrm_ref[...])


def _ssd_decode(z, xbc, dtg, conv_state, ssm_state, sp):
    b = z.shape[0]
    assert 2 * b <= LANES and b % 8 == 0
    _, _, ex = _ssd_consts()
    y, st, cs = pl.pallas_call(
        _ssd_decode_kernel,
        out_shape=[jax.ShapeDtypeStruct((b, M_INNER), F32), jax.ShapeDtypeStruct((b, M_INNER, M_STATE), F32),
                   jax.ShapeDtypeStruct((CONV_W - 1, b, M_CONV_DIM), F32)],
        compiler_params=pltpu.CompilerParams(vmem_limit_bytes=48 * 1024 * 1024),
        name="ssd_decode",
    )(z, xbc, dtg, conv_state.transpose(1, 0, 2), ssm_state.reshape(b, M_INNER, M_STATE),
      sp["cw"], sp["cb"], sp["dtb"], sp["alog"], sp["dskip"], sp["nrm"], ex)
    return y, st.reshape(b, M_HEADS, M_HD, M_STATE), cs.transpose(1, 0, 2)


CMP_ROWS = 2 * C_HD
ROWS_PER_STEP = 8
PAGE_TILE = 512


def _cmp_params(P, l):
    w = P["nsa_cmp_w"][l].transpose(0, 2, 1, 3)
    zeros = jnp.zeros_like(w[0])
    lanes_k = jnp.concatenate([w[0], zeros], axis=-1)
    lanes_v = jnp.concatenate([zeros, w[1]], axis=-1)
    per_blk = jnp.concatenate([lanes_k, lanes_v], axis=0)
    z = jnp.zeros_like(per_blk)
    wn = jnp.concatenate([jnp.concatenate([per_blk, z], axis=-1),
                          jnp.concatenate([z, per_blk], axis=-1)], axis=1).astype(BF16)
    pe = P["nsa_cmp_pe"][l].transpose(1, 2, 0).reshape(CMP_ROWS, CMP_BLOCK)
    kn = P["nsa_k_norm"][l][0]
    one = jnp.ones((C_HD,), F32)
    return wn, jnp.tile(pe, (1, 2)), jnp.concatenate([kn, one, kn, one])[None, :]


def _compress_kernel(x_ref, pe_ref, w_ref, g64_ref, kn_ref, o_ref, acc_sc):
    r = pl.program_id(1)

    @pl.when(r == 0)
    def _():
        acc_sc[...] = jnp.zeros_like(acc_sc)

    acc = acc_sc[...]
    tp = x_ref.shape[0]
    rows = x_ref.reshape(tp * ROWS_PER_STEP, LANES)
    for dd in range(ROWS_PER_STEP):
        x = rows[pl.ds(dd, tp, stride=ROWS_PER_STEP), :]
        acc = acc + _dot((x + pe_ref[dd:dd + 1, :]).astype(BF16), w_ref[dd])
    acc_sc[...] = acc

    @pl.when(r == pl.num_programs(1) - 1)
    def _():
        lane = lax.broadcasted_iota(I32, acc.shape, 1)
        o_ref[...] = jnp.where((lane & C_HD) == 0, _group_norm(acc, g64_ref[...]) * kn_ref[...], acc)


def _compress(x_t, l, cmp_p, g64):
    n = x_t.shape[1]
    tp = n if n <= PAGE_TILE else PAGE_TILE
    assert n % tp == 0
    wn, pe, kn = cmp_p
    return pl.pallas_call(
        _compress_kernel,
        grid=(n // tp, CMP_ROWS // ROWS_PER_STEP),
        in_specs=[pl.BlockSpec((None, tp, ROWS_PER_STEP, LANES), lambda i, r: (l, i, r, 0)),
                  pl.BlockSpec((ROWS_PER_STEP, LANES), lambda i, r: (r, 0)),
                  pl.BlockSpec((ROWS_PER_STEP, LANES, 2 * LANES), lambda i, r: (r, 0, 0)),
                  pl.BlockSpec(g64.shape, lambda i, r: (0, 0)),
                  pl.BlockSpec(kn.shape, lambda i, r: (0, 0))],
        out_specs=pl.BlockSpec((tp, 2 * LANES), lambda i, r: (i, 0)),
        out_shape=jax.ShapeDtypeStruct((n, 2 * LANES), F32),
        scratch_shapes=[pltpu.VMEM((tp, 2 * LANES), F32)],
        compiler_params=_cparams(("parallel", "arbitrary"), 40),
        name="nsa_compress",
    )(x_t, pe, wn, g64, kn)


def _pick_step(ids, axis, on_pick):
    def body(it, carry):
        vals, state = carry
        m = jnp.max(vals, axis=axis, keepdims=True)
        first = jnp.min(jnp.where(vals == m, ids, 1e9), axis=axis, keepdims=True)
        pick = (ids == first) & (m >= 0.0)
        return jnp.where(pick, -1.0, vals), on_pick(it, pick, state)

    return body


def _nsa_select_kernel(q_ref, kc_ref, vcT_ref, ocmp_ref, sel_ref, *, tq):
    i = pl.program_id(0)
    kc = kc_ref[...]
    vcT = vcT_ref[...]
    nb = kc.shape[0]
    blk = lax.broadcasted_iota(I32, (nb, tq), 0)
    pos = i * tq + lax.broadcasted_iota(I32, (nb, tq), 1)
    cmask = (blk + 1) * CMP_BLOCK - 1 <= pos
    imp = jnp.zeros((nb, tq), F32)
    for h in range(C_HEADS):
        s = jnp.where(cmask, _dot_nt(kc, q_ref[h]), NEG)
        e = jnp.where(cmask, jnp.exp2(s - jnp.max(s, axis=0, keepdims=True)), 0.0)
        den = jnp.sum(e, axis=0, keepdims=True)
        p = e / jnp.where(den > 0.0, den, 1.0)
        ocmp_ref[h] = _dot(vcT, p.astype(BF16))
        imp = imp + p
    cur = lax.shift_right_logical(pos, int(math.log2(SEL_BLOCK)))
    vals = jnp.where(blk < cur, imp, -1.0)
    sel = jnp.where(blk == cur, 1.0, 0.0)
    body = _pick_step(blk.astype(F32), 0, lambda it, pick, sel: jnp.where(pick, 1.0, sel))
    _, sel = lax.fori_loop(0, N_SEL - 1, body, (vals, sel))
    sel_ref[...] = sel.astype(BF16)


def _nsa_select(q, kc, vcT, tq):
    s = q.shape[1]
    nb = kc.shape[0]
    return pl.pallas_call(
        functools.partial(_nsa_select_kernel, tq=tq),
        grid=(s // tq,),
        in_specs=[pl.BlockSpec((C_HEADS, tq, C_HD), lambda i: (0, i, 0)), pl.BlockSpec((nb, C_HD), lambda i: (0, 0)),
                  pl.BlockSpec((C_HD, nb), lambda i: (0, 0))],
        out_specs=[pl.BlockSpec((C_HEADS, C_HD, tq), lambda i: (0, 0, i)), pl.BlockSpec((nb, tq), lambda i: (0, i))],
        out_shape=[jax.ShapeDtypeStruct((C_HEADS, C_HD, s), F32), jax.ShapeDtypeStruct((nb, s), BF16)],
        compiler_params=_cparams(("parallel",), 40),
        name="nsa_select",
    )(q, kc, vcT)


def _flash_update_t(k, q, vT, valid, m_ref, l_ref, acc_ref, h):
    s = jnp.where(valid, _dot_nt(k, q), NEG)
    m_prev = m_ref[h]
    m_new = jnp.maximum(m_prev, jnp.max(s, axis=0, keepdims=True))
    alpha = jnp.exp2(m_prev - m_new)
    p = jnp.where(valid, jnp.exp2(s - m_new), 0.0)
    l_ref[h] = alpha * l_ref[h] + jnp.sum(p, axis=0, keepdims=True)
    acc_ref[h] = alpha * acc_ref[h] + _dot(vT, p.astype(BF16))
    m_ref[h] = m_new


def _nsa_attn_kernel(qi_ref, kj_ref, lf_ref, wj_ref, q_ref, ks_ref, vsT_ref, kw_ref, vwT_ref, sel_ref, ocmp_ref, g_ref,
                     o_ref, ms_sc, ls_sc, as_sc, mw_sc, lw_sc, aw_sc, *, tq, tk):
    t = pl.program_id(0)
    i = qi_ref[t]
    j = kj_ref[t]

    @pl.when(j == 0)
    def _():
        for m_sc, l_sc, a_sc in ((ms_sc, ls_sc, as_sc), (mw_sc, lw_sc, aw_sc)):
            m_sc[...] = jnp.full(m_sc.shape, NEG, F32)
            l_sc[...] = jnp.zeros_like(l_sc)
            a_sc[...] = jnp.zeros_like(a_sc)

    kpos = j * tk + lax.broadcasted_iota(I32, (tk, tq), 0)
    qpos = i * tq + lax.broadcasted_iota(I32, (tk, tq), 1)
    nb = sel_ref.shape[0]
    key_blk = lax.shift_right_logical(j * tk + lax.broadcasted_iota(I32, (tk, nb), 0), int(math.log2(SEL_BLOCK)))
    expand = jnp.where(lax.broadcasted_iota(I32, (tk, nb), 1) == key_blk, 1.0, 0.0).astype(BF16)
    picked = _dot(expand, sel_ref[...])
    valid = (picked > 0.5) & (kpos <= qpos)
    ks = ks_ref[...]
    vsT = vsT_ref[...]
    for h in range(C_HEADS):
        _flash_update_t(ks, q_ref[h], vsT, valid, ms_sc, ls_sc, as_sc, h)

    @pl.when(wj_ref[t] == j)
    def _():
        wvalid = (kpos <= qpos) & (kpos >= qpos - WINDOW)
        kw = kw_ref[...]
        vwT = vwT_ref[...]
        for h in range(C_HEADS):
            _flash_update_t(kw, q_ref[h], vwT, wvalid, mw_sc, lw_sc, aw_sc, h)

    @pl.when(lf_ref[t] == 1)
    def _():
        g = g_ref[...]
        for h in range(C_HEADS):
            gate = lambda br: g[br * C_HEADS + h:br * C_HEADS + h + 1, :]
            o_ref[h] = gate(0) * ocmp_ref[h] + gate(1) * (as_sc[h] / ls_sc[h]) + gate(2) * (aw_sc[h] / lw_sc[h])


def _nsa_attn(q, ks, vsT, kw, vwT, selT, ocmpT, gT, tq, tk):
    s = q.shape[1]
    nb = selT.shape[0]
    assert tk % tq == 0 and s % tk == 0 and WINDOW % tk == 0
    qi, kj, lf, wj = _causal_pairs(s // tq, tq, tk, WINDOW)
    tq_lanes = lambda rows: pl.BlockSpec((rows, tq), lambda t, qi, kj, lf, wj: (0, qi[t]))
    hd_q = pl.BlockSpec((C_HEADS, C_HD, tq), lambda t, qi, kj, lf, wj: (0, 0, qi[t]))
    grid_spec = pltpu.PrefetchScalarGridSpec(
        num_scalar_prefetch=4,
        grid=(qi.shape[0],),
        in_specs=[
            pl.BlockSpec((C_HEADS, tq, C_HD), lambda t, qi, kj, lf, wj: (0, qi[t], 0)),
            pl.BlockSpec((tk, C_HD), lambda t, qi, kj, lf, wj: (kj[t], 0)),
            pl.BlockSpec((C_HD, tk), lambda t, qi, kj, lf, wj: (0, kj[t])),
            pl.BlockSpec((tk, C_HD), lambda t, qi, kj, lf, wj: (wj[t], 0)),
            pl.BlockSpec((C_HD, tk), lambda t, qi, kj, lf, wj: (0, wj[t])),
            tq_lanes(nb), hd_q, tq_lanes(N_GATE),
        ],
        out_specs=hd_q,
        scratch_shapes=[pltpu.VMEM((C_HEADS, 1, tq), F32), pltpu.VMEM((C_HEADS, 1, tq), F32),
                        pltpu.VMEM((C_HEADS, C_HD, tq), F32)] * 2,
    )
    return pl.pallas_call(
        functools.partial(_nsa_attn_kernel, tq=tq, tk=tk),
        grid_spec=grid_spec,
        out_shape=jax.ShapeDtypeStruct((C_HEADS, C_HD, s), F32),
        compiler_params=_cparams(("arbitrary",), 48),
        name="nsa_attn",
    )(qi, kj, lf, wj, q, ks, vsT, kw, vwT, selT, ocmpT, gT)


def _nsa_prompt(cq, nsa, nsab, winb, dtg, cmp_p, g64):
    s = cq.shape[0]
    q = cq.reshape(s, C_HEADS, C_HD).transpose(1, 0, 2)
    x_t = nsa[:, 0:CMP_ROWS].reshape(s // LANES, LANES, CMP_ROWS).transpose(0, 2, 1)[None]
    kcv = _compress(x_t, 0, cmp_p, g64).reshape(s // CMP_BLOCK, 2 * C_HD)
    ocmpT, selT = _nsa_select(q, kcv[:, :C_HD].astype(BF16), kcv[:, C_HD:].T.astype(BF16), tq=128)
    oT = _nsa_attn(q, nsab[:, 128:192], nsab[:, 192:256].T, winb[:, 0:C_HD], winb[:, C_HD:].T, selT, ocmpT,
                   dtg[:, N_DT:N_DT + N_GATE].T, tq=ATTN_TQ, tk=ATTN_TK)
    return oT.transpose(2, 0, 1).reshape(s, C_HEADS * C_HD)


SEQ_PER_STEP = 8
FLAG_LANE = 16


def _nsa_dec1_kernel(pt_ref, q_ref, ptv_ref, kcvp_ref, ocmp_ref, phys_ref, buf, sem, imp_sc, *, n_pages):
    g = pl.program_id(0)

    def row_copy(sb, pg):
        page = pt_ref[g * SEQ_PER_STEP + sb, pg]
        return pltpu.make_async_copy(kcvp_ref.at[pl.ds(page, 1), :], buf.at[sb, pl.ds(pg, 1), :], sem.at[0])

    def start_all(pg, carry):
        for sb in range(SEQ_PER_STEP):
            row_copy(sb, pg).start()
        return carry

    def wait_all(pg, carry):
        for sb in range(SEQ_PER_STEP):
            row_copy(sb, pg).wait()
        return carry

    lax.fori_loop(0, n_pages, start_all, 0)
    lax.fori_loop(0, n_pages, wait_all, 0)

    row8 = lax.broadcasted_iota(I32, (8, 2 * n_pages), 0)
    for sb in range(SEQ_PER_STEP):
        kcv = buf[sb].astype(BF16)
        s01 = _dot_nt(q_ref[sb].astype(BF16), kcv)
        s = jnp.concatenate([s01[0:8], s01[8:16]], axis=1)
        e = jnp.exp2(s - jnp.max(s, axis=-1, keepdims=True))
        p = e / jnp.sum(e, axis=-1, keepdims=True)
        r0 = _dot(p[:, 0:n_pages].astype(BF16), kcv)
        r1 = _dot(p[:, n_pages:].astype(BF16), kcv)
        ocmp_ref[sb] = r0[:, 0:LANES] + r1[:, LANES:]
        imp_sc[sb:sb + 1, :] = jnp.sum(jnp.where(row8 < C_HEADS, p, 0.0), axis=0, keepdims=True)

    col = lax.broadcasted_iota(I32, (SEQ_PER_STEP, 2 * n_pages), 1)
    second = col >= n_pages
    ids = (2 * jnp.where(second, col - n_pages, col) + jnp.where(second, 1, 0)).astype(F32)
    ptv = ptv_ref[...].astype(F32)
    phys_all = jnp.concatenate([2.0 * ptv, 2.0 * ptv + 1.0], axis=1)
    lane = lax.broadcasted_iota(I32, (SEQ_PER_STEP, LANES), 1)

    def on_pick(it, pick, acc):
        ph = jnp.sum(jnp.where(pick, phys_all, 0.0), axis=-1, keepdims=True)
        ok = jnp.sum(jnp.where(pick, 1.0, 0.0), axis=-1, keepdims=True)
        return jnp.where(lane == it, ph, jnp.where(lane == FLAG_LANE + it, ok, acc))

    _, acc = lax.fori_loop(0, N_SEL - 1, _pick_step(ids, -1, on_pick),
                           (imp_sc[...], jnp.zeros((SEQ_PER_STEP, LANES), F32)))
    phys_ref[...] = acc.astype(I32)


def _nsa_dec1(q01, page_table, kcvp):
    b, n_pages = page_table.shape
    assert b % SEQ_PER_STEP == 0
    grid_spec = pltpu.PrefetchScalarGridSpec(
        num_scalar_prefetch=1,
        grid=(b // SEQ_PER_STEP,),
        in_specs=[pl.BlockSpec((SEQ_PER_STEP, 16, 256), lambda g, pt: (g, 0, 0)),
                  pl.BlockSpec((SEQ_PER_STEP, n_pages), lambda g, pt: (g, 0)),
                  pl.BlockSpec(memory_space=pl.ANY)],
        out_specs=[pl.BlockSpec((SEQ_PER_STEP, 8, LANES), lambda g, pt: (g, 0, 0)),
                   pl.BlockSpec((SEQ_PER_STEP, LANES), lambda g, pt: (g, 0))],
        scratch_shapes=[pltpu.VMEM((SEQ_PER_STEP, n_pages, 256), F32), pltpu.SemaphoreType.DMA((1,)),
                        pltpu.VMEM((SEQ_PER_STEP, 2 * n_pages), F32)],
    )
    return pl.pallas_call(
        functools.partial(_nsa_dec1_kernel, n_pages=n_pages),
        grid_spec=grid_spec,
        out_shape=[jax.ShapeDtypeStruct((b, 8, LANES), F32), jax.ShapeDtypeStruct((b, LANES), I32)],
        compiler_params=_cparams(("arbitrary",), 32),
        name="nsa_decode_select",
    )(page_table, q01, page_table, kcvp)


def _nsa_dec2_kernel(ph_ref, q_ref, *rest, n_blk):
    blk_refs = rest[:n_blk]
    ws_ref, sn_ref, wn_ref, oc_ref, gm_ref, o_ref = rest[n_blk:]
    b = pl.program_id(0)
    q = q_ref[...]
    qb = q.astype(BF16)

    def attend(slabs, valid, new_row):
        slabs = [sl.astype(BF16) for sl in slabs]
        s = jnp.concatenate([_dot(qb, sl[0:C_HD, :]) for sl in slabs], axis=1)
        s_new = jnp.sum(q * new_row[:, 0:C_HD], axis=-1, keepdims=True)
        if valid is not None:
            s = jnp.where(valid, s, NEG)
        m = jnp.maximum(jnp.max(s, axis=-1, keepdims=True), s_new)
        p = jnp.exp2(s - m)
        if valid is not None:
            p = jnp.where(valid, p, 0.0)
        pn = jnp.exp2(s_new - m)
        den = jnp.sum(p, axis=-1, keepdims=True) + pn
        p = p.astype(BF16)
        pv = pn * new_row
        off = 0
        for sl in slabs:
            pv = pv + _dot_nt(p[:, off:off + sl.shape[1]], sl)
            off += sl.shape[1]
        return pv / den

    page = blk_refs[0].shape[1]
    shift = int(math.log2(SEL_BLOCK))
    col = lax.broadcasted_iota(I32, (8, n_blk * page), 1)
    col_slab = lax.shift_right_logical(col, int(math.log2(page)))
    col_half = lax.shift_right_logical(col, shift) & (page // SEL_BLOCK - 1)
    vf = jnp.zeros((8, n_blk * page), F32)
    for j in range(n_blk):
        half = ph_ref[b, j] & (page // SEL_BLOCK - 1)
        flag = ph_ref[b, FLAG_LANE + j].astype(F32)
        vf = jnp.where((col_slab == j) & (col_half == half), flag, vf)
    o_sel = attend([r[...] for r in blk_refs], vf > 0.5, sn_ref[...])
    o_win = attend([ws_ref[...]], None, wn_ref[...])
    gm = gm_ref[...]
    o_ref[...] = gm[:, 0:1] * oc_ref[...] + gm[:, 1:2] * o_sel + gm[:, 2:3] * o_win


def _nsa_dec2(phys, q8, pool_t, win_t, l, selnew, winnew, ocmp, gm):
    b = q8.shape[0]
    n_blk = N_SEL - 1
    page, wlen = pool_t.shape[3], win_t.shape[3]
    bpp = page // SEL_BLOCK
    per_b = lambda shape: pl.BlockSpec((None,) + shape, lambda bi, ph: (bi, 0, 0))
    blk_specs = [pl.BlockSpec((None, None, 2 * C_HD, page),
                              functools.partial(lambda bi, ph, j: (l, ph[bi, j] // bpp, 1, 0), j=j)) for j in range(n_blk)]
    grid_spec = pltpu.PrefetchScalarGridSpec(
        num_scalar_prefetch=1,
        grid=(b,),
        in_specs=[per_b((8, C_HD))] + blk_specs +
                 [pl.BlockSpec((None, None, 2 * C_HD, wlen), lambda bi, ph: (l, bi, 0, 0)),
                  per_b((1, LANES)), per_b((1, LANES)), per_b((8, LANES)), per_b((8, LANES))],
        out_specs=per_b((8, LANES)),
    )
    o = pl.pallas_call(
        functools.partial(_nsa_dec2_kernel, n_blk=n_blk),
        grid_spec=grid_spec,
        out_shape=jax.ShapeDtypeStruct((b, 8, LANES), F32),
        compiler_params=_cparams(("arbitrary",), 32),
        name="nsa_decode_attend",
    )(phys, q8, *([pool_t] * n_blk), win_t, selnew, winnew, ocmp, gm)
    return o[:, 0:C_HEADS, C_HD:].reshape(b, C_HEADS * C_HD)


def _nsa_decode(cq, nsa_new, win_new, dtg, pool_t, win_t, l, page_table, cmp_p, g64):
    b = cq.shape[0]
    assert pool_t.shape[3] == 2 * CMP_BLOCK and CMP_BLOCK == SEL_BLOCK
    kcvp = _compress(pool_t, l, cmp_p, g64)
    q4 = cq.astype(F32).reshape(b, C_HEADS, C_HD)
    q_first = jnp.pad(q4, ((0, 0), (0, 8 - C_HEADS), (0, 2 * LANES - C_HD)))
    q_second = jnp.pad(q4, ((0, 0), (0, 8 - C_HEADS), (LANES, LANES - C_HD)))
    ocmp, phys = _nsa_dec1(jnp.concatenate([q_first, q_second], axis=1), page_table, kcvp)
    q8 = jnp.pad(q4, ((0, 0), (0, 8 - C_HEADS), (0, 0)))
    gates = dtg[:, N_DT:N_DT + N_GATE].reshape(b, 3, C_HEADS).transpose(0, 2, 1)
    gm = jnp.pad(gates, ((0, 0), (0, 8 - C_HEADS), (0, LANES - 3)))
    return _nsa_dec2(phys, q8, pool_t, win_t, l, nsa_new[:, None, LANES:], win_new[:, None, :], ocmp, gm)


def kernel(x_prompt, x_sample, cache_diff_kv, cache_nsa_kv, state_nsa_win, state_ssm, state_conv, page_table, ffn1_norm, ffn1_w_gate, ffn1_w_up, ffn1_w_down, mix_norm, w_in, diff_q_norm, diff_k_norm, diff_lambda, diff_subln, conv_w, conv_b, dt_bias, a_log, d_skip, ssm_norm, nsa_q_norm, nsa_k_norm, nsa_cmp_pe, nsa_cmp_w, w_out, ffn2_norm, ffn2_w_gate, ffn2_w_up, ffn2_w_down):
    P = dict(ffn1_norm=ffn1_norm, ffn1_w_gate=ffn1_w_gate, ffn1_w_up=ffn1_w_up, ffn1_w_down=ffn1_w_down, mix_norm=mix_norm,
             w_in=w_in, diff_q_norm=diff_q_norm, diff_k_norm=diff_k_norm, conv_w=conv_w, conv_b=conv_b, dt_bias=dt_bias,
             a_log=a_log, d_skip=d_skip, ssm_norm=ssm_norm, nsa_q_norm=nsa_q_norm, nsa_k_norm=nsa_k_norm,
             nsa_cmp_pe=nsa_cmp_pe, nsa_cmp_w=nsa_cmp_w, w_out=w_out, ffn2_norm=ffn2_norm, ffn2_w_gate=ffn2_w_gate,
             ffn2_w_up=ffn2_w_up, ffn2_w_down=ffn2_w_down)
    depth = w_in.shape[0]
    xp = x_prompt[0]
    xs = x_sample[:, 0]
    s, b = xp.shape[0], xs.shape[0]
    n_pool, page = cache_diff_kv.shape[1], cache_diff_kv.shape[2]
    tm = min(ROW_TILE, s)
    wlen = min(WINDOW, s)
    diff_t = cache_diff_kv.transpose(0, 1, 3, 4, 5, 2).reshape(depth, n_pool, 2 * A_HEADS * A_HD, page)
    nsa_t = cache_nsa_kv.transpose(0, 1, 3, 4, 2).reshape(depth, n_pool, 4 * C_HD, page)
    win_t = state_nsa_win.transpose(0, 1, 3, 4, 2).reshape(depth, b, 2 * C_HD, state_nsa_win.shape[2])
    outs = [[] for _ in range(10)]
    for l in range(depth):
        lam_init = 0.8 - 0.6 * math.exp(-0.3 * l)
        p = _prep_layer(l, P)
        sp = _ssd_params(P, l)
        cmp_p = _cmp_params(P, l)
        dl = diff_lambda[l]
        subg = diff_subln[l]
        xp = _ffn(xp, *p["ffn1"], tm=tm)
        xs = _ffn(xs, *p["ffn1"], tm=b)
        qd, dkv, dkvb, z, xbc, dtg, cq, nsa, nsab, win, winb = _inproj(xp, p, tm=tm)
        od = _diff_prompt(qd, dkvb, dl, subg[None, :], lam_init, tq=ATTN_TQ, tk=ATTN_TK)
        osm, ssm_f = _ssd_prompt(z, xbc, dtg, sp)
        on = _nsa_prompt(cq, nsa, nsab, winb, dtg, cmp_p, p["g64"])
        xp = _mixout(xp, od, osm, on, p["w_out"], tm=tm)
        prompt_outs = (dkv.reshape(1, s, 2, A_HEADS, A_HD), nsa.reshape(1, s, 4, C_HD),
                       win[s - wlen:].reshape(1, wlen, 2, C_HD), ssm_f[None], xbc[s - (CONV_W - 1):][None])
        sqd, sdkv, _, sz, sxbc, sdtg, scq, snsa, _, swin, _ = _inproj(xs, p, tm=b)
        sod = _diff_decode(sqd, sdkv, diff_t, l, page_table, dl, jnp.tile(subg, A_HEADS)[None, :], lam_init,
                           n_pg=DECODE_PAGES_PER_STEP)
        sos, sst, scs = _ssd_decode(sz, sxbc, sdtg, state_conv[l], state_ssm[l], sp)
        son = _nsa_decode(scq, snsa, swin, sdtg, nsa_t, win_t, l, page_table, cmp_p, p["g64"])
        new_win = jnp.concatenate([state_nsa_win[l][:, 1:], swin.reshape(b, 1, 2, C_HD)], axis=1)
        xs = _mixout(xs, sod, sos, son, p["w_out"], tm=b)
        sample_outs = (sdkv.reshape(b, 1, 2, A_HEADS, A_HD), snsa.reshape(b, 1, 4, C_HD), new_win, sst, scs)
        for acc, o in zip(outs, prompt_outs + sample_outs):
            acc.append(o)
        xp = _ffn(xp, *p["ffn2"], tm=tm)
        xs = _ffn(xs, *p["ffn2"], tm=b)
    return (xp[None], xs[:, None]) + tuple(jnp.stack(o) for o in outs)
```

```python
import functools
import math

import jax
import jax.numpy as jnp
import numpy as np
from jax import lax
from jax.experimental import pallas as pl
from jax.experimental.pallas import tpu as pltpu

F32, BF16, I32 = jnp.float32, jnp.bfloat16, jnp.int32

A_HEADS, A_QK, A_HD = 4, 32, 64
M_HEADS, M_HD, M_INNER, M_GROUPS, M_STATE, CONV_W, M_CONV_DIM, M_CHUNK = 8, 64, 512, 2, 128, 4, 1024, 128
C_HEADS, C_HD, CMP_BLOCK, SEL_BLOCK, N_SEL, WINDOW = 4, 64, 64, 64, 16, 512
EPS = 1e-6
NEG = -1e30
M_INIT = -1e29
ONES_ROWS = 8
LOG2E = math.log2(math.e)
LANES = 128
OFF_AQ, OFF_AK, OFF_AV, OFF_Z, OFF_XBC, OFF_CQ, OFF_CKV, OFF_DTG, W_IN_COLS = 0, 256, 512, 768, 1280, 2304, 2560, 2944, 3072
N_DT = M_HEADS
N_GATE = 3 * C_HEADS
ROW_TILE = 512
ATTN_TQ, ATTN_TK = 512, 1024
DECODE_PAGES_PER_STEP = 16


def _cparams(sem, vmem_mib):
    return pltpu.CompilerParams(dimension_semantics=sem, vmem_limit_bytes=vmem_mib * 1024 * 1024)


def _dot(a, b):
    return jnp.dot(a, b, preferred_element_type=F32)


def _dot_nt(a, b):
    return lax.dot_general(a, b, (((1,), (1,)), ((), ())), preferred_element_type=F32)


def _split3(x):
    x1 = x.astype(BF16)
    r = x - x1.astype(F32)
    x2 = r.astype(BF16)
    x3 = (r - x2.astype(F32)).astype(BF16)
    return x1, x2, x3


def _dot_exact_rhs(x, a_bf):
    x1, x2, x3 = _split3(x)
    return _dot(x1, a_bf) + _dot(x2, a_bf) + _dot(x3, a_bf)


def _dot_exact_lhs(a_bf, x):
    x1, x2, x3 = _split3(x)
    return _dot(a_bf, x1) + _dot(a_bf, x2) + _dot(a_bf, x3)


def _sigmoid(x):
    return 1.0 / (1.0 + jnp.exp(-x))


def _silu(x):
    return x * _sigmoid(x)


def _softplus(x):
    return jnp.maximum(x, 0.0) + jnp.log1p(jnp.exp(-jnp.abs(x)))


def _rms_rows(x, g):
    return x * lax.rsqrt(jnp.mean(x * x, axis=-1, keepdims=True) + EPS) * g


def _group_norm(v, gmat_bf):
    sq = v * v
    hi = sq.astype(BF16)
    lo = (sq - hi.astype(F32)).astype(BF16)
    ms = _dot(hi, gmat_bf) + _dot(lo, gmat_bf)
    return v * lax.rsqrt(ms + EPS)


def _ffn_kernel(x_ref, g_ref, wg_ref, wu_ref, wd_ref, o_ref, h_sc, acc_sc):
    c = pl.program_id(1)

    @pl.when(c == 0)
    def _():
        h_sc[...] = _rms_rows(x_ref[...], g_ref[...]).astype(BF16)
        acc_sc[...] = jnp.zeros_like(acc_sc)

    h = h_sc[...]
    a = _dot(h, wg_ref[...])
    b = _dot(h, wu_ref[...])
    acc_sc[...] += _dot((_silu(a) * b).astype(BF16), wd_ref[...])

    @pl.when(c == pl.num_programs(1) - 1)
    def _():
        o_ref[...] = x_ref[...] + 0.5 * acc_sc[...]


def _ffn(x, g, wg, wu, wd, tm):
    m, d = x.shape
    ff = wg.shape[1]
    fc = ff // 2
    return pl.pallas_call(
        _ffn_kernel,
        grid=(m // tm, ff // fc),
        in_specs=[
            pl.BlockSpec((tm, d), lambda i, c: (i, 0)),
            pl.BlockSpec((1, d), lambda i, c: (0, 0)),
            pl.BlockSpec((d, fc), lambda i, c: (0, c)),
            pl.BlockSpec((d, fc), lambda i, c: (0, c)),
            pl.BlockSpec((fc, d), lambda i, c: (c, 0)),
        ],
        out_specs=pl.BlockSpec((tm, d), lambda i, c: (i, 0)),
        out_shape=jax.ShapeDtypeStruct((m, d), F32),
        scratch_shapes=[pltpu.VMEM((tm, d), BF16), pltpu.VMEM((tm, d), F32)],
        compiler_params=_cparams(("parallel", "arbitrary"), 56),
        name="ffn",
    )(x, g, wg, wu, wd)


def _inproj_kernel(x_ref, g_ref, w_ref, g32_ref, g64_ref, qg_ref, kg_ref, cqg_ref, ng_ref, nm_ref, wg_ref, wm_ref,
                   qd_ref, dkv_ref, dkvb_ref, z_ref, xbc_ref, dtg_ref, cq_ref, nsa_ref, nsab_ref,
                   win_ref, winb_ref):
    h = _rms_rows(x_ref[...], g_ref[...]).astype(BF16)

    def seg(a, b):
        return _dot(h, w_ref[:, a:b])

    g32 = g32_ref[...]
    g64 = g64_ref[...]
    qd_ref[...] = (_group_norm(seg(OFF_AQ, OFF_AK), g32) * qg_ref[...]).astype(BF16)
    kn = _group_norm(seg(OFF_AK, OFF_AV), g32) * kg_ref[...]
    av = seg(OFF_AV, OFF_Z)
    dkv_ref[:, 0:256] = kn
    dkv_ref[:, 256:512] = av
    dkvb_ref[:, 0:256] = kn.astype(BF16)
    dkvb_ref[:, 256:512] = av.astype(BF16)
    z_ref[...] = seg(OFF_Z, OFF_XBC)
    xbc_ref[...] = seg(OFF_XBC, OFF_CQ)
    cq_ref[...] = (_group_norm(seg(OFF_CQ, OFF_CKV), g64) * cqg_ref[...]).astype(BF16)
    nr = seg(OFF_CKV, OFF_CKV + 256)
    nsa = jnp.where(nm_ref[...] > 0.5, _group_norm(nr, g64) * ng_ref[...], nr)
    nsa_ref[...] = nsa
    nsab_ref[...] = nsa.astype(BF16)
    wr = seg(OFF_CKV + 256, OFF_DTG)
    win = jnp.where(wm_ref[...] > 0.5, _group_norm(wr, g64_ref[0:128, 0:128]) * wg_ref[...], wr)
    win_ref[...] = win
    winb_ref[...] = win.astype(BF16)
    d = seg(OFF_DTG, W_IN_COLS)
    lane = lax.broadcasted_iota(I32, d.shape, 1)
    dtg_ref[...] = jnp.where(lane >= N_DT, _sigmoid(d), d)


def _inproj(x, p, tm):
    m, d = x.shape
    row = lambda w: pl.BlockSpec((tm, w), lambda i: (i, 0))
    full = lambda a: pl.BlockSpec(a.shape, lambda i: (0,) * a.ndim)
    consts = [p["mix_norm"], p["w_in"], p["g32"], p["g64"], p["qg"], p["kg"], p["cqg"], p["ng"], p["nm"], p["wg"], p["wm"]]
    outs = [(256, BF16), (512, F32), (512, BF16), (512, F32), (1024, F32), (128, F32), (256, BF16),
            (256, F32), (256, BF16), (128, F32), (128, BF16)]
    return pl.pallas_call(
        _inproj_kernel,
        grid=(m // tm,),
        in_specs=[row(d)] + [full(a) for a in consts],
        out_specs=[row(w) for w, _ in outs],
        out_shape=[jax.ShapeDtypeStruct((m, w), dt) for w, dt in outs],
        compiler_params=_cparams(("parallel",), 56),
        name="inproj",
    )(x, *consts)


def _mixout_kernel(x_ref, od_ref, os_ref, on_ref, w_ref, o_ref):
    o_ref[...] = (x_ref[...]
                  + _dot(od_ref[...].astype(BF16), w_ref[0:256, :])
                  + _dot(os_ref[...].astype(BF16), w_ref[256:768, :])
                  + _dot(on_ref[...].astype(BF16), w_ref[768:1024, :]))


def _mixout(x, od, osm, on, w, tm):
    m, d = x.shape
    row = lambda w_: pl.BlockSpec((tm, w_), lambda i: (i, 0))
    return pl.pallas_call(
        _mixout_kernel,
        grid=(m // tm,),
        in_specs=[row(d), row(256), row(512), row(256), pl.BlockSpec(w.shape, lambda i: (0, 0))],
        out_specs=row(d),
        out_shape=jax.ShapeDtypeStruct((m, d), F32),
        compiler_params=_cparams(("parallel",), 40),
        name="mixout",
    )(x, od, osm, on, w)


IN_SPLITS = (256, 256, 256, 512, 1024, 8, 256, 384, 12)


def _block_diag_mean(width, group):
    m = np.zeros((width, width), np.float32)
    for s in range(0, width, group):
        m[s:s + group, s:s + group] = 1.0 / group
    return jnp.asarray(m, BF16)


def _prep_layer(l, P):
    d = P["w_in"].shape[1]
    offs = np.cumsum((0,) + IN_SPLITS)
    cols = [P["w_in"][l][:, offs[i]:offs[i + 1]] for i in range(len(IN_SPLITS))]
    aq, ak, av, z, xbc, dt, cq, ckv, cg = cols
    cg = cg.reshape(d, C_HEADS, 3).transpose(0, 2, 1).reshape(d, N_GATE)
    dtg = jnp.concatenate([dt, cg, jnp.zeros((d, LANES - N_DT - N_GATE), F32)], axis=1)
    w_in = jnp.concatenate([aq, ak, av, z, xbc, cq, ckv, dtg], axis=1).astype(BF16)
    ones64 = jnp.ones((C_HD,), F32)
    kn = P["nsa_k_norm"][l]
    p = {
        "mix_norm": P["mix_norm"][l][None, :],
        "w_in": w_in,
        "g32": _block_diag_mean(256, A_QK),
        "g64": _block_diag_mean(256, C_HD),
        "qg": (jnp.tile(P["diff_q_norm"][l], 2 * A_HEADS) * (A_QK ** -0.5 * LOG2E))[None, :],
        "kg": jnp.tile(P["diff_k_norm"][l], 2 * A_HEADS)[None, :],
        "cqg": (jnp.tile(P["nsa_q_norm"][l], C_HEADS) * (C_HD ** -0.5 * LOG2E))[None, :],
        "ng": jnp.concatenate([ones64, ones64, kn[1], ones64])[None, :],
        "nm": jnp.concatenate([0 * ones64, 0 * ones64, ones64, 0 * ones64])[None, :],
        "wg": jnp.concatenate([kn[2], ones64])[None, :],
        "wm": jnp.concatenate([ones64, 0 * ones64])[None, :],
    }
    for f in ("ffn1", "ffn2"):
        p[f] = (P[f + "_norm"][l][None, :], P[f + "_w_gate"][l].astype(BF16), P[f + "_w_up"][l].astype(BF16),
                P[f + "_w_down"][l].astype(BF16))
    p["w_out"] = P["w_out"][l].astype(BF16)
    return p


def _causal_pairs(n_q, tq, tk, window=None):
    qi, kj, lf, wj = [], [], [], []
    for i in range(n_q):
        last = (i * tq + tq - 1) // tk
        first_w = max(0, (i * tq - (window or 0)) // tk)
        for j in range(last + 1):
            qi.append(i), kj.append(j), lf.append(int(j == last)), wj.append(max(j, first_w))
    return tuple(jnp.asarray(np.asarray(a, np.int32)) for a in (qi, kj, lf, wj))


def _diff_lambda(dl, lam_init):
    a = jnp.sum(dl[0:1] * dl[1:2], axis=-1, keepdims=True)
    b = jnp.sum(dl[2:3] * dl[3:4], axis=-1, keepdims=True)
    return jnp.exp(a) - jnp.exp(b) + lam_init


def _diff_prompt_kernel(qi_ref, kj_ref, lf_ref, q_ref, k_ref, vT_ref, dl_ref, sg_ref, o_ref,
                        qm_sc, m_sc, acc_sc, s_sc, p_sc, *, tq, tk, lam_init):
    t = pl.program_id(0)
    i = qi_ref[t]
    j = kj_ref[t]
    n_hc = 2 * A_HEADS

    @pl.when(j == 0)
    def _():
        q = q_ref[...]
        lane = lax.broadcasted_iota(I32, q.shape, 1)
        for hc in range(n_hc):
            qm_sc[hc] = jnp.where((lane >= A_QK * hc) & (lane < A_QK * (hc + 1)), q, jnp.zeros_like(q))
        m_sc[...] = jnp.full(m_sc.shape, M_INIT, F32)
        acc_sc[...] = jnp.zeros_like(acc_sc)

    def sweep(masked):
        k = k_ref[...]
        if masked:
            kpos = j * tk + lax.broadcasted_iota(I32, (tk, tq), 0)
            qpos = i * tq + lax.broadcasted_iota(I32, (tk, tq), 1)
            keep = kpos <= qpos
        ms = [m_sc[hc] for hc in range(n_hc)]
        new_m = []
        for hc in range(n_hc):
            s = _dot_nt(k, qm_sc[hc])
            if masked:
                s = jnp.where(keep, s, NEG)
            s_sc[hc] = s
            new_m.append(jnp.maximum(ms[hc], jnp.max(s, axis=0, keepdims=True)))
        for hc in range(n_hc):
            p_sc[hc] = jnp.exp2(s_sc[hc] - new_m[hc]).astype(BF16)
        for hc in range(n_hc):
            alpha = jnp.exp2(ms[hc] - new_m[hc])
            acc_sc[hc] = alpha * acc_sc[hc] + _dot(vT_ref[hc // 2], p_sc[hc])
            m_sc[hc] = new_m[hc]

    diag = (j * tk + tk - 1) > (i * tq)
    pl.when(diag)(lambda: sweep(True))
    pl.when(jnp.logical_not(diag))(lambda: sweep(False))

    @pl.when(lf_ref[t] == 1)
    def _():
        lam = _diff_lambda(dl_ref[...], lam_init)
        for h in range(A_HEADS):
            a0, a1 = acc_sc[2 * h], acc_sc[2 * h + 1]
            o = a0[0:A_HD] / a0[A_HD:A_HD + 1] - lam * (a1[0:A_HD] / a1[A_HD:A_HD + 1])
            ms = jnp.mean(o * o, axis=0, keepdims=True)
            o_ref[h] = o * lax.rsqrt(ms + EPS) * sg_ref[...] * (1.0 - lam_init)


def _with_ones_rows(vT):
    return jnp.concatenate([vT, jnp.ones(vT.shape[:-2] + (ONES_ROWS, vT.shape[-1]), vT.dtype)], axis=-2)


def _diff_prompt(qd, dkvb, dl, subg, lam_init, tq, tk):
    s = qd.shape[0]
    assert tk % tq == 0 and s % tk == 0
    vT = _with_ones_rows(dkvb[:, 256:].reshape(s, A_HEADS, A_HD).transpose(1, 2, 0))
    qi, kj, lf, _ = _causal_pairs(s // tq, tq, tk)
    grid_spec = pltpu.PrefetchScalarGridSpec(
        num_scalar_prefetch=3,
        grid=(qi.shape[0],),
        in_specs=[
            pl.BlockSpec((tq, 256), lambda t, qi, kj, lf: (qi[t], 0)),
            pl.BlockSpec((tk, 256), lambda t, qi, kj, lf: (kj[t], 0)),
            pl.BlockSpec((A_HEADS, A_HD + ONES_ROWS, tk), lambda t, qi, kj, lf: (0, 0, kj[t])),
            pl.BlockSpec((4, A_QK), lambda t, qi, kj, lf: (0, 0)),
            pl.BlockSpec((A_HD, 1), lambda t, qi, kj, lf: (0, 0)),
        ],
        out_specs=pl.BlockSpec((A_HEADS, A_HD, tq), lambda t, qi, kj, lf: (0, 0, qi[t])),
        scratch_shapes=[pltpu.VMEM((2 * A_HEADS, tq, 256), BF16), pltpu.VMEM((2 * A_HEADS, 1, tq), F32),
                        pltpu.VMEM((2 * A_HEADS, A_HD + ONES_ROWS, tq), F32),
                        pltpu.VMEM((2 * A_HEADS, tk, tq), F32), pltpu.VMEM((2 * A_HEADS, tk, tq), BF16)],
    )
    o = pl.pallas_call(
        functools.partial(_diff_prompt_kernel, tq=tq, tk=tk, lam_init=lam_init),
        grid_spec=grid_spec,
        out_shape=jax.ShapeDtypeStruct((A_HEADS, A_HD, s), F32),
        compiler_params=_cparams(("arbitrary",), 48),
        name="diff_prompt",
    )(qi, kj, lf, qd, dkvb, vT, dl, subg.reshape(A_HD, 1))
    return o.transpose(2, 0, 1).reshape(s, A_HEADS * A_HD)


def _diff_decode_kernel(pt_ref, q_ref, new_ref, *rest, n_pg, lam_init):
    page_refs = rest[:n_pg]
    dl_ref, sg_ref, o_ref, m_sc, l_sc, acc_sc = rest[n_pg:]
    step = pl.program_id(1)
    n_hc = 2 * A_HEADS
    lane = lax.broadcasted_iota(I32, (n_hc, 256), 1)
    row = lax.broadcasted_iota(I32, (n_hc, 256), 0)
    qm = jnp.where((lane >= A_QK * row) & (lane < A_QK * (row + 1)), q_ref[0], 0.0)

    @pl.when(step == 0)
    def _():
        new = new_ref[0]
        m_sc[...] = jnp.sum(qm * new[:, 0:256], axis=-1, keepdims=True)
        l_sc[...] = jnp.ones_like(l_sc)
        acc_sc[...] = jnp.broadcast_to(new[:, 256:512], acc_sc.shape)

    qb = qm.astype(BF16)
    n_kd = 2 * A_HEADS * A_QK
    s = jnp.concatenate([_dot(qb, r[0:n_kd, :].astype(BF16)) for r in page_refs], axis=1)
    m_prev = m_sc[...]
    m_new = jnp.maximum(m_prev, jnp.max(s, axis=-1, keepdims=True))
    alpha = jnp.exp2(m_prev - m_new)
    p = jnp.exp2(s - m_new)
    l_sc[...] = alpha * l_sc[...] + jnp.sum(p, axis=-1, keepdims=True)
    p = p.astype(BF16)
    page = page_refs[0].shape[1]
    pv = _dot_nt(p[:, 0:page], page_refs[0][n_kd:, :].astype(BF16))
    for g in range(1, n_pg):
        pv = pv + _dot_nt(p[:, g * page:(g + 1) * page], page_refs[g][n_kd:, :].astype(BF16))
    acc_sc[...] = alpha * acc_sc[...] + pv
    m_sc[...] = m_new

    @pl.when(step == pl.num_programs(1) - 1)
    def _():
        lam = _diff_lambda(dl_ref[...], lam_init)
        a = acc_sc[...] / l_sc[...]
        lane1 = lax.broadcasted_iota(I32, (1, 256), 1)
        o = jnp.zeros((1, 256), F32)
        for h in range(A_HEADS):
            in_h = (lane1 >= A_HD * h) & (lane1 < A_HD * (h + 1))
            oh = a[2 * h:2 * h + 1, :] - lam * a[2 * h + 1:2 * h + 2, :]
            ms = jnp.sum(jnp.where(in_h, oh * oh, 0.0), axis=-1, keepdims=True) * (1.0 / A_HD)
            o = jnp.where(in_h, oh * lax.rsqrt(ms + EPS), o)
        o_ref[0] = o * sg_ref[...] * (1.0 - lam_init)


def _diff_decode(qd, dkv_new, pool_t, l, page_table, dl, subg4, lam_init, n_pg):
    b, n_pages = page_table.shape
    page = pool_t.shape[3]
    assert n_pages % n_pg == 0
    cmap = lambda bi, si, pt: (0, 0)
    page_specs = [pl.BlockSpec((None, None, 512, page),
                               functools.partial(lambda bi, si, pt, g: (l, pt[bi, si * n_pg + g], 0, 0), g=g))
                  for g in range(n_pg)]
    grid_spec = pltpu.PrefetchScalarGridSpec(
        num_scalar_prefetch=1,
        grid=(b, n_pages // n_pg),
        in_specs=[pl.BlockSpec((1, 1, 256), lambda bi, si, pt: (bi, 0, 0)),
                  pl.BlockSpec((1, 1, 512), lambda bi, si, pt: (bi, 0, 0))] + page_specs +
                 [pl.BlockSpec((4, A_QK), cmap), pl.BlockSpec((1, 256), cmap)],
        out_specs=pl.BlockSpec((1, 1, 256), lambda bi, si, pt: (bi, 0, 0)),
        scratch_shapes=[pltpu.VMEM((2 * A_HEADS, 1), F32), pltpu.VMEM((2 * A_HEADS, 1), F32),
                        pltpu.VMEM((2 * A_HEADS, 256), F32)],
    )
    o = pl.pallas_call(
        functools.partial(_diff_decode_kernel, n_pg=n_pg, lam_init=lam_init),
        grid_spec=grid_spec,
        out_shape=jax.ShapeDtypeStruct((b, 1, 256), F32),
        compiler_params=_cparams(("parallel", "arbitrary"), 48),
        name="diff_decode",
    )(page_table, qd.astype(F32).reshape(b, 1, 256), dkv_new.reshape(b, 1, 512), *([pool_t] * n_pg), dl, subg4)
    return o.reshape(b, 256)


def _ssd_consts():
    tri = np.tril(np.ones((M_CHUNK, M_CHUNK), np.float32))
    expand = np.zeros((LANES, M_INNER), np.float32)
    for h in range(M_HEADS):
        expand[h, h * M_HD:(h + 1) * M_HD] = 1.0
    return jnp.asarray(tri, BF16), jnp.asarray(tri.T, BF16), jnp.asarray(expand, BF16)


def _ssd_params(P, l):
    pad = lambda v: jnp.concatenate([v, jnp.zeros((LANES - M_HEADS,), F32)])[None, :]
    return dict(cw=P["conv_w"][l], cb=P["conv_b"][l][None, :], dtb=pad(P["dt_bias"][l]), dtbT=P["dt_bias"][l][:, None],
                alog=pad(P["a_log"][l]), alogT=P["a_log"][l][:, None],
                dskip=jnp.repeat(P["d_skip"][l], M_HD)[None, :], nrm=P["ssm_norm"][l][None, :])


def _gated_group_norm(y, x, z, dskip, nrm):
    g = (y + x * dskip) * _silu(z)
    gw = M_INNER // M_GROUPS
    return jnp.concatenate([_rms_rows(g[:, i * gw:(i + 1) * gw], nrm[:, i * gw:(i + 1) * gw]) for i in range(M_GROUPS)], axis=1)


def _ssd_prompt_kernel(z_ref, xbc_ref, dtg_ref, dtT_ref, cw_ref, cb_ref, dtb_ref, dtbT_ref, alog_ref, alogT_ref,
                       dskip_ref, nrm_ref, tri_ref, triT_ref, ex_ref, y_ref, hT_ref, xpad_sc, h_sc):
    c = pl.program_id(0)
    q = M_CHUNK

    @pl.when(c == 0)
    def _():
        xpad_sc[0:8, :] = jnp.zeros((8, M_CONV_DIM), F32)
        h_sc[...] = jnp.zeros_like(h_sc)

    xpad_sc[8:8 + q, :] = xbc_ref[...]
    conv = cb_ref[...]
    for j in range(CONV_W):
        conv = conv + cw_ref[j:j + 1, :] * xpad_sc[8 - (CONV_W - 1) + j:8 - (CONV_W - 1) + j + q, :]
    xpad_sc[0:8, :] = xbc_ref[q - 8:q, :]
    xa = _silu(conv)
    x = xa[:, 0:M_INNER]
    bm = xa[:, M_INNER:M_INNER + M_GROUPS * M_STATE]
    cm = xa[:, M_INNER + M_GROUPS * M_STATE:]

    lane = lax.broadcasted_iota(I32, (1, LANES), 1)
    is_head = lane < M_HEADS
    a_row = jnp.where(is_head, -jnp.exp(alog_ref[...]), 0.0)
    dt = jnp.where(is_head, _softplus(dtg_ref[...] + dtb_ref[...]), 0.0)
    acum = _dot_exact_lhs(tri_ref[...], dt * a_row)
    daT = _softplus(dtT_ref[...] + dtbT_ref[...]) * (-jnp.exp(alogT_ref[...]))
    acumT = _dot_exact_rhs(daT, triT_ref[...])
    ex = ex_ref[...]
    acum_x = _dot_exact_rhs(acum, ex)
    dt_x = _dot_exact_rhs(dt, ex)
    last = acum_x[q - 1:q, :]
    xdt = x * dt_x
    xdd = (xdt * jnp.exp(last - acum_x)).astype(BF16)
    h_prev = h_sc[...]

    rowi = lax.broadcasted_iota(I32, (q, q), 0)
    coli = lax.broadcasted_iota(I32, (q, q), 1)
    causal = rowi >= coli
    lane_q = lax.broadcasted_iota(I32, (q, LANES), 1)
    hpg = M_HEADS // M_GROUPS
    gw = hpg * M_HD
    y_diag, y_off, st = [], [], []
    for g in range(M_GROUPS):
        c_g = cm[:, g * M_STATE:(g + 1) * M_STATE].astype(BF16)
        bT_g = bm[:, g * M_STATE:(g + 1) * M_STATE].T.astype(BF16)
        y_off.append(_dot(c_g, h_prev[:, g * gw:(g + 1) * gw].astype(BF16)))
        st.append(_dot(bT_g, xdd[:, g * gw:(g + 1) * gw]))
        cb = _dot(c_g, bT_g)
        for pr in range(hpg // 2):
            ms = []
            for hh in range(2):
                h = g * hpg + pr * 2 + hh
                seg = acum[:, h:h + 1] - acumT[h:h + 1, :]
                ms.append(cb * jnp.where(causal, jnp.exp(jnp.where(causal, seg, 0.0)), 0.0))
            xp = xdt[:, (g * hpg + pr * 2) * M_HD:(g * hpg + pr * 2 + 2) * M_HD]
            rhs = jnp.concatenate([jnp.where(lane_q < M_HD, xp, 0.0), jnp.where(lane_q >= M_HD, xp, 0.0)], axis=0)
            y_diag.append(_dot(jnp.concatenate(ms, axis=1).astype(BF16), rhs.astype(BF16)))
    y = jnp.concatenate(y_diag, axis=1) + jnp.concatenate(y_off, axis=1) * jnp.exp(acum_x)
    h_new = h_prev * jnp.exp(last) + jnp.concatenate(st, axis=1)
    h_sc[...] = h_new
    y_ref[...] = _gated_group_norm(y, x, z_ref[...], dskip_ref[...], nrm_ref[...])

    @pl.when(c == pl.num_programs(0) - 1)
    def _():
        hT_ref[...] = h_new


def _ssd_prompt(z, xbc, dtg, sp):
    s = z.shape[0]
    q = M_CHUNK
    tri, triT, ex = _ssd_consts()
    dtT = dtg[:, :M_HEADS].T
    consts = [sp["cw"], sp["cb"], sp["dtb"], sp["dtbT"], sp["alog"], sp["alogT"], sp["dskip"], sp["nrm"], tri, triT, ex]
    row = lambda w: pl.BlockSpec((q, w), lambda c: (c, 0))
    y, hT = pl.pallas_call(
        _ssd_prompt_kernel,
        grid=(s // q,),
        in_specs=[row(M_INNER), row(M_CONV_DIM), row(LANES), pl.BlockSpec((M_HEADS, q), lambda c: (0, c))]
                 + [pl.BlockSpec(a.shape, lambda c: (0, 0)) for a in consts],
        out_specs=[row(M_INNER), pl.BlockSpec((M_STATE, M_INNER), lambda c: (0, 0))],
        out_shape=[jax.ShapeDtypeStruct((s, M_INNER), F32), jax.ShapeDtypeStruct((M_STATE, M_INNER), F32)],
        scratch_shapes=[pltpu.VMEM((8 + q, M_CONV_DIM), F32), pltpu.VMEM((M_STATE, M_INNER), F32)],
        compiler_params=_cparams(("arbitrary",), 40),
        name="ssd_prompt",
    )(z, xbc, dtg, dtT, *consts)
    return y, hT.reshape(M_STATE, M_HEADS, M_HD).transpose(1, 2, 0)


def _ssd_decode_kernel(z_ref, xbc_ref, dtg_ref, cs_ref, st_ref, cw_ref, cb_ref, dtb_ref, alog_ref, dskip_ref, nrm_ref,
                       ex_ref, y_ref, sto_ref, cso_ref):
    b = z_ref.shape[0]
    xbc = xbc_ref[...]
    conv = cb_ref[...] + cw_ref[CONV_W - 1:CONV_W, :] * xbc
    for j in range(CONV_W - 1):
        conv = conv + cw_ref[j:j + 1, :] * cs_ref[j]
        if j > 0:
            cso_ref[j - 1] = cs_ref[j]
    cso_ref[CONV_W - 2] = xbc
    xa = _silu(conv)
    x = xa[:, 0:M_INNER]
    bm = xa[:, M_INNER:M_INNER + M_GROUPS * M_STATE]
    cm = xa[:, M_INNER + M_GROUPS * M_STATE:]
    lane = lax.broadcasted_iota(I32, (1, LANES), 1)
    is_head = lane < M_HEADS
    dt = jnp.where(is_head, _softplus(dtg_ref[...] + dtb_ref[...]), 0.0)
    da = dt * jnp.where(is_head, -jnp.exp(alog_ref[...]), 0.0)
    ex = ex_ref[...]
    xdt = x * _dot_exact_rhs(dt, ex)
    dec = jnp.exp(_dot_exact_rhs(da, ex))
    stack = jnp.concatenate([xdt, dec, jnp.zeros((LANES - 2 * b, M_INNER), F32)], axis=0)
    cols = stack.T
    lane_c = lax.broadcasted_iota(I32, (M_INNER, LANES), 1)
    ycols = jnp.zeros((M_INNER, LANES), F32)
    gw = M_INNER // M_GROUPS
    for s in range(b):
        xcol = cols[:, s:s + 1]
        dcol = cols[:, b + s:b + s + 1]
        h = st_ref[s]
        parts = []
        for g in range(M_GROUPS):
            rows = slice(g * gw, (g + 1) * gw)
            hn = h[rows] * dcol[rows] + xcol[rows] * bm[s:s + 1, g * M_STATE:(g + 1) * M_STATE]
            sto_ref[s, rows, :] = hn
            parts.append(jnp.sum(hn * cm[s:s + 1, g * M_STATE:(g + 1) * M_STATE], axis=-1, keepdims=True))
        ycols = jnp.where(lane_c == s, jnp.concatenate(parts, axis=0), ycols)
    y = ycols.T[0:b, :]
    y_ref[...] = _gated_group_norm(y, x, z_ref[...], dskip_ref[...], nrm_ref[...])


def _ssd_decode(z, xbc, dtg, conv_state, ssm_state, sp):
    b = z.shape[0]
    assert 2 * b <= LANES and b % 8 == 0
    _, _, ex = _ssd_consts()
    y, st, cs = pl.pallas_call(
        _ssd_decode_kernel,
        out_shape=[jax.ShapeDtypeStruct((b, M_INNER), F32), jax.ShapeDtypeStruct((b, M_INNER, M_STATE), F32),
                   jax.ShapeDtypeStruct((CONV_W - 1, b, M_CONV_DIM), F32)],
        compiler_params=pltpu.CompilerParams(vmem_limit_bytes=48 * 1024 * 1024),
        name="ssd_decode",
    )(z, xbc, dtg, conv_state.transpose(1, 0, 2), ssm_state.reshape(b, M_INNER, M_STATE),
      sp["cw"], sp["cb"], sp["dtb"], sp["alog"], sp["dskip"], sp["nrm"], ex)
    return y, st.reshape(b, M_HEADS, M_HD, M_STATE), cs.transpose(1, 0, 2)


CMP_ROWS = 2 * C_HD
ROWS_PER_STEP = 8
PAGE_TILE = 512


def _cmp_params(P, l):
    w = P["nsa_cmp_w"][l].transpose(0, 2, 1, 3)
    zeros = jnp.zeros_like(w[0])
    lanes_k = jnp.concatenate([w[0], zeros], axis=-1)
    lanes_v = jnp.concatenate([zeros, w[1]], axis=-1)
    per_blk = jnp.concatenate([lanes_k, lanes_v], axis=0)
    z = jnp.zeros_like(per_blk)
    wn = jnp.concatenate([jnp.concatenate([per_blk, z], axis=-1),
                          jnp.concatenate([z, per_blk], axis=-1)], axis=1).astype(BF16)
    pe = P["nsa_cmp_pe"][l].transpose(1, 2, 0).reshape(CMP_ROWS, CMP_BLOCK)
    kn = P["nsa_k_norm"][l][0]
    one = jnp.ones((C_HD,), F32)
    return wn, jnp.tile(pe, (1, 2)), jnp.concatenate([kn, one, kn, one])[None, :]


def _compress_kernel(x_ref, pe_ref, w_ref, g64_ref, kn_ref, o_ref, acc_sc):
    r = pl.program_id(1)

    @pl.when(r == 0)
    def _():
        acc_sc[...] = jnp.zeros_like(acc_sc)

    acc = acc_sc[...]
    tp = x_ref.shape[0]
    rows = x_ref.reshape(tp * ROWS_PER_STEP, LANES)
    for dd in range(ROWS_PER_STEP):
        x = rows[pl.ds(dd, tp, stride=ROWS_PER_STEP), :]
        acc = acc + _dot((x + pe_ref[dd:dd + 1, :]).astype(BF16), w_ref[dd])
    acc_sc[...] = acc

    @pl.when(r == pl.num_programs(1) - 1)
    def _():
        lane = lax.broadcasted_iota(I32, acc.shape, 1)
        o_ref[...] = jnp.where((lane & C_HD) == 0, _group_norm(acc, g64_ref[...]) * kn_ref[...], acc)


def _compress(x_t, l, cmp_p, g64):
    n = x_t.shape[1]
    tp = n if n <= PAGE_TILE else PAGE_TILE
    assert n % tp == 0
    wn, pe, kn = cmp_p
    return pl.pallas_call(
        _compress_kernel,
        grid=(n // tp, CMP_ROWS // ROWS_PER_STEP),
        in_specs=[pl.BlockSpec((None, tp, ROWS_PER_STEP, LANES), lambda i, r: (l, i, r, 0)),
                  pl.BlockSpec((ROWS_PER_STEP, LANES), lambda i, r: (r, 0)),
                  pl.BlockSpec((ROWS_PER_STEP, LANES, 2 * LANES), lambda i, r: (r, 0, 0)),
                  pl.BlockSpec(g64.shape, lambda i, r: (0, 0)),
                  pl.BlockSpec(kn.shape, lambda i, r: (0, 0))],
        out_specs=pl.BlockSpec((tp, 2 * LANES), lambda i, r: (i, 0)),
        out_shape=jax.ShapeDtypeStruct((n, 2 * LANES), F32),
        scratch_shapes=[pltpu.VMEM((tp, 2 * LANES), F32)],
        compiler_params=_cparams(("parallel", "arbitrary"), 40),
        name="nsa_compress",
    )(x_t, pe, wn, g64, kn)


def _pick_step(ids, axis, on_pick):
    def body(it, carry):
        vals, state = carry
        m = jnp.max(vals, axis=axis, keepdims=True)
        first = jnp.min(jnp.where(vals == m, ids, 1e9), axis=axis, keepdims=True)
        pick = (ids == first) & (m >= 0.0)
        return jnp.where(pick, -1.0, vals), on_pick(it, pick, state)

    return body


def _nsa_select_kernel(q_ref, kc_ref, vcT_ref, ocmp_ref, sel_ref, *, tq):
    i = pl.program_id(0)
    kc = kc_ref[...]
    vcT = vcT_ref[...]
    nb = kc.shape[0]
    blk = lax.broadcasted_iota(I32, (nb, tq), 0)
    pos = i * tq + lax.broadcasted_iota(I32, (nb, tq), 1)
    cmask = (blk + 1) * CMP_BLOCK - 1 <= pos
    imp = jnp.zeros((nb, tq), F32)
    for h in range(C_HEADS):
        s = jnp.where(cmask, _dot_nt(kc, q_ref[h]), NEG)
        e = jnp.where(cmask, jnp.exp2(s - jnp.max(s, axis=0, keepdims=True)), 0.0)
        den = jnp.sum(e, axis=0, keepdims=True)
        p = e / jnp.where(den > 0.0, den, 1.0)
        ocmp_ref[h] = _dot(vcT, p.astype(BF16))
        imp = imp + p
    cur = lax.shift_right_logical(pos, int(math.log2(SEL_BLOCK)))
    vals = jnp.where(blk < cur, imp, -1.0)
    sel = jnp.where(blk == cur, 1.0, 0.0)
    body = _pick_step(blk.astype(F32), 0, lambda it, pick, sel: jnp.where(pick, 1.0, sel))
    _, sel = lax.fori_loop(0, N_SEL - 1, body, (vals, sel))
    sel_ref[...] = sel.astype(BF16)


def _nsa_select(q, kc, vcT, tq):
    s = q.shape[1]
    nb = kc.shape[0]
    return pl.pallas_call(
        functools.partial(_nsa_select_kernel, tq=tq),
        grid=(s // tq,),
        in_specs=[pl.BlockSpec((C_HEADS, tq, C_HD), lambda i: (0, i, 0)), pl.BlockSpec((nb, C_HD), lambda i: (0, 0)),
                  pl.BlockSpec((C_HD, nb), lambda i: (0, 0))],
        out_specs=[pl.BlockSpec((C_HEADS, C_HD, tq), lambda i: (0, 0, i)), pl.BlockSpec((nb, tq), lambda i: (0, i))],
        out_shape=[jax.ShapeDtypeStruct((C_HEADS, C_HD, s), F32), jax.ShapeDtypeStruct((nb, s), BF16)],
        compiler_params=_cparams(("parallel",), 40),
        name="nsa_select",
    )(q, kc, vcT)


def _nsa_attn_kernel(qi_ref, kj_ref, lf_ref, wj_ref, q_ref, ks_ref, vsT_ref, kw_ref, vwT_ref, sel_ref, ocmp_ref, g_ref,
                     o_ref, m_sc, acc_sc, s_sc, p_sc, *, tq, tk):
    t = pl.program_id(0)
    i = qi_ref[t]
    j = kj_ref[t]
    nh = C_HEADS

    @pl.when(j == 0)
    def _():
        m_sc[...] = jnp.full(m_sc.shape, M_INIT, F32)
        acc_sc[...] = jnp.zeros_like(acc_sc)

    kpos = j * tk + lax.broadcasted_iota(I32, (tk, tq), 0)
    qpos = i * tq + lax.broadcasted_iota(I32, (tk, tq), 1)
    nb = sel_ref.shape[0]
    key_blk = lax.shift_right_logical(j * tk + lax.broadcasted_iota(I32, (tk, nb), 0), int(math.log2(SEL_BLOCK)))
    expand = jnp.where(lax.broadcasted_iota(I32, (tk, nb), 1) == key_blk, 1.0, 0.0).astype(BF16)
    picked = _dot(expand, sel_ref[...])
    valid = (picked > 0.5) & (kpos <= qpos)

    def branch(k, vT, ok, base):
        ms = [m_sc[base + h] for h in range(nh)]
        new_m = []
        for h in range(nh):
            s = jnp.where(ok, _dot_nt(k, q_ref[h]), NEG)
            s_sc[h] = s
            new_m.append(jnp.maximum(ms[h], jnp.max(s, axis=0, keepdims=True)))
        for h in range(nh):
            p_sc[h] = jnp.exp2(s_sc[h] - new_m[h]).astype(BF16)
        for h in range(nh):
            alpha = jnp.exp2(ms[h] - new_m[h])
            acc_sc[base + h] = alpha * acc_sc[base + h] + _dot(vT, p_sc[h])
            m_sc[base + h] = new_m[h]

    branch(ks_ref[...], vsT_ref[...], valid, 0)

    @pl.when(wj_ref[t] == j)
    def _():
        branch(kw_ref[...], vwT_ref[...], (kpos <= qpos) & (kpos >= qpos - WINDOW), nh)

    @pl.when(lf_ref[t] == 1)
    def _():
        g = g_ref[...]
        for h in range(nh):
            gate = lambda br: g[br * nh + h:br * nh + h + 1, :]
            a_s, a_w = acc_sc[h], acc_sc[nh + h]
            o_ref[h] = (gate(0) * ocmp_ref[h] + gate(1) * (a_s[0:C_HD] / a_s[C_HD:C_HD + 1])
                        + gate(2) * (a_w[0:C_HD] / a_w[C_HD:C_HD + 1]))


def _nsa_attn(q, ks, vsT, kw, vwT, selT, ocmpT, gT, tq, tk):
    s = q.shape[1]
    nb = selT.shape[0]
    assert tk % tq == 0 and s % tk == 0 and (WINDOW % tk == 0 or tk % WINDOW == 0)
    qi, kj, lf, wj = _causal_pairs(s // tq, tq, tk, WINDOW)
    tq_lanes = lambda rows: pl.BlockSpec((rows, tq), lambda t, qi, kj, lf, wj: (0, qi[t]))
    hd_q = pl.BlockSpec((C_HEADS, C_HD, tq), lambda t, qi, kj, lf, wj: (0, 0, qi[t]))
    grid_spec = pltpu.PrefetchScalarGridSpec(
        num_scalar_prefetch=4,
        grid=(qi.shape[0],),
        in_specs=[
            pl.BlockSpec((C_HEADS, tq, C_HD), lambda t, qi, kj, lf, wj: (0, qi[t], 0)),
            pl.BlockSpec((tk, C_HD), lambda t, qi, kj, lf, wj: (kj[t], 0)),
            pl.BlockSpec((C_HD + ONES_ROWS, tk), lambda t, qi, kj, lf, wj: (0, kj[t])),
            pl.BlockSpec((tk, C_HD), lambda t, qi, kj, lf, wj: (wj[t], 0)),
            pl.BlockSpec((C_HD + ONES_ROWS, tk), lambda t, qi, kj, lf, wj: (0, wj[t])),
            tq_lanes(nb), hd_q, tq_lanes(N_GATE),
        ],
        out_specs=hd_q,
        scratch_shapes=[pltpu.VMEM((2 * C_HEADS, 1, tq), F32), pltpu.VMEM((2 * C_HEADS, C_HD + ONES_ROWS, tq), F32),
                        pltpu.VMEM((C_HEADS, tk, tq), F32), pltpu.VMEM((C_HEADS, tk, tq), BF16)],
    )
    return pl.pallas_call(
        functools.partial(_nsa_attn_kernel, tq=tq, tk=tk),
        grid_spec=grid_spec,
        out_shape=jax.ShapeDtypeStruct((C_HEADS, C_HD, s), F32),
        compiler_params=_cparams(("arbitrary",), 48),
        name="nsa_attn",
    )(qi, kj, lf, wj, q, ks, _with_ones_rows(vsT), kw, _with_ones_rows(vwT), selT, ocmpT, gT)


def _nsa_prompt(cq, nsa, nsab, winb, dtg, cmp_p, g64):
    s = cq.shape[0]
    q = cq.reshape(s, C_HEADS, C_HD).transpose(1, 0, 2)
    x_t = nsa[:, 0:CMP_ROWS].reshape(s // LANES, LANES, CMP_ROWS).transpose(0, 2, 1)[None]
    kcv = _compress(x_t, 0, cmp_p, g64).reshape(s // CMP_BLOCK, 2 * C_HD)
    ocmpT, selT = _nsa_select(q, kcv[:, :C_HD].astype(BF16), kcv[:, C_HD:].T.astype(BF16), tq=128)
    oT = _nsa_attn(q, nsab[:, 128:192], nsab[:, 192:256].T, winb[:, 0:C_HD], winb[:, C_HD:].T, selT, ocmpT,
                   dtg[:, N_DT:N_DT + N_GATE].T, tq=ATTN_TQ, tk=ATTN_TK)
    return oT.transpose(2, 0, 1).reshape(s, C_HEADS * C_HD)


SEQ_PER_STEP = 8
FLAG_LANE = 16


def _nsa_dec1_kernel(pt_ref, q_ref, ptv_ref, kcvp_ref, ocmp_ref, phys_ref, buf, sem, imp_sc, *, n_pages):
    g = pl.program_id(0)

    def row_copy(sb, pg):
        page = pt_ref[g * SEQ_PER_STEP + sb, pg]
        return pltpu.make_async_copy(kcvp_ref.at[pl.ds(page, 1), :], buf.at[sb, pl.ds(pg, 1), :], sem.at[0])

    def start_all(pg, carry):
        for sb in range(SEQ_PER_STEP):
            row_copy(sb, pg).start()
        return carry

    def wait_all(pg, carry):
        for sb in range(SEQ_PER_STEP):
            row_copy(sb, pg).wait()
        return carry

    lax.fori_loop(0, n_pages, start_all, 0)
    lax.fori_loop(0, n_pages, wait_all, 0)

    row8 = lax.broadcasted_iota(I32, (8, 2 * n_pages), 0)
    for sb in range(SEQ_PER_STEP):
        kcv = buf[sb].astype(BF16)
        s01 = _dot_nt(q_ref[sb].astype(BF16), kcv)
        s = jnp.concatenate([s01[0:8], s01[8:16]], axis=1)
        e = jnp.exp2(s - jnp.max(s, axis=-1, keepdims=True))
        p = e / jnp.sum(e, axis=-1, keepdims=True)
        r0 = _dot(p[:, 0:n_pages].astype(BF16), kcv)
        r1 = _dot(p[:, n_pages:].astype(BF16), kcv)
        ocmp_ref[sb] = r0[:, 0:LANES] + r1[:, LANES:]
        imp_sc[sb:sb + 1, :] = jnp.sum(jnp.where(row8 < C_HEADS, p, 0.0), axis=0, keepdims=True)

    col = lax.broadcasted_iota(I32, (SEQ_PER_STEP, 2 * n_pages), 1)
    second = col >= n_pages
    ids = (2 * jnp.where(second, col - n_pages, col) + jnp.where(second, 1, 0)).astype(F32)
    ptv = ptv_ref[...].astype(F32)
    phys_all = jnp.concatenate([2.0 * ptv, 2.0 * ptv + 1.0], axis=1)
    lane = lax.broadcasted_iota(I32, (SEQ_PER_STEP, LANES), 1)

    def on_pick(it, pick, acc):
        ph = jnp.sum(jnp.where(pick, phys_all, 0.0), axis=-1, keepdims=True)
        ok = jnp.sum(jnp.where(pick, 1.0, 0.0), axis=-1, keepdims=True)
        return jnp.where(lane == it, ph, jnp.where(lane == FLAG_LANE + it, ok, acc))

    _, acc = lax.fori_loop(0, N_SEL - 1, _pick_step(ids, -1, on_pick),
                           (imp_sc[...], jnp.zeros((SEQ_PER_STEP, LANES), F32)))
    phys_ref[...] = acc.astype(I32)


def _nsa_dec1(q01, page_table, kcvp):
    b, n_pages = page_table.shape
    assert b % SEQ_PER_STEP == 0
    grid_spec = pltpu.PrefetchScalarGridSpec(
        num_scalar_prefetch=1,
        grid=(b // SEQ_PER_STEP,),
        in_specs=[pl.BlockSpec((SEQ_PER_STEP, 16, 256), lambda g, pt: (g, 0, 0)),
                  pl.BlockSpec((SEQ_PER_STEP, n_pages), lambda g, pt: (g, 0)),
                  pl.BlockSpec(memory_space=pl.ANY)],
        out_specs=[pl.BlockSpec((SEQ_PER_STEP, 8, LANES), lambda g, pt: (g, 0, 0)),
                   pl.BlockSpec((SEQ_PER_STEP, LANES), lambda g, pt: (g, 0))],
        scratch_shapes=[pltpu.VMEM((SEQ_PER_STEP, n_pages, 256), F32), pltpu.SemaphoreType.DMA((1,)),
                        pltpu.VMEM((SEQ_PER_STEP, 2 * n_pages), F32)],
    )
    return pl.pallas_call(
        functools.partial(_nsa_dec1_kernel, n_pages=n_pages),
        grid_spec=grid_spec,
        out_shape=[jax.ShapeDtypeStruct((b, 8, LANES), F32), jax.ShapeDtypeStruct((b, LANES), I32)],
        compiler_params=_cparams(("arbitrary",), 32),
        name="nsa_decode_select",
    )(page_table, q01, page_table, kcvp)


def _nsa_dec2_kernel(ph_ref, q_ref, *rest, n_blk):
    blk_refs = rest[:n_blk]
    ws_ref, sn_ref, wn_ref, oc_ref, gm_ref, o_ref = rest[n_blk:]
    b = pl.program_id(0)
    q = q_ref[...]
    qb = q.astype(BF16)

    def attend(slabs, valid, new_row):
        slabs = [sl.astype(BF16) for sl in slabs]
        s = jnp.concatenate([_dot(qb, sl[0:C_HD, :]) for sl in slabs], axis=1)
        s_new = jnp.sum(q * new_row[:, 0:C_HD], axis=-1, keepdims=True)
        if valid is not None:
            s = jnp.where(valid, s, NEG)
        m = jnp.maximum(jnp.max(s, axis=-1, keepdims=True), s_new)
        p = jnp.exp2(s - m)
        if valid is not None:
            p = jnp.where(valid, p, 0.0)
        pn = jnp.exp2(s_new - m)
        den = jnp.sum(p, axis=-1, keepdims=True) + pn
        p = p.astype(BF16)
        pv = pn * new_row
        off = 0
        for sl in slabs:
            pv = pv + _dot_nt(p[:, off:off + sl.shape[1]], sl)
            off += sl.shape[1]
        return pv / den

    page = blk_refs[0].shape[1]
    shift = int(math.log2(SEL_BLOCK))
    col = lax.broadcasted_iota(I32, (8, n_blk * page), 1)
    col_slab = lax.shift_right_logical(col, int(math.log2(page)))
    col_half = lax.shift_right_logical(col, shift) & (page // SEL_BLOCK - 1)
    vf = jnp.zeros((8, n_blk * page), F32)
    for j in range(n_blk):
        half = ph_ref[b, j] & (page // SEL_BLOCK - 1)
        flag = ph_ref[b, FLAG_LANE + j].astype(F32)
        vf = jnp.where((col_slab == j) & (col_half == half), flag, vf)
    o_sel = attend([r[...] for r in blk_refs], vf > 0.5, sn_ref[...])
    o_win = attend([ws_ref[...]], None, wn_ref[...])
    gm = gm_ref[...]
    o_ref[...] = gm[:, 0:1] * oc_ref[...] + gm[:, 1:2] * o_sel + gm[:, 2:3] * o_win


def _nsa_dec2(phys, q8, pool_t, win_t, l, selnew, winnew, ocmp, gm):
    b = q8.shape[0]
    n_blk = N_SEL - 1
    page, wlen = pool_t.shape[3], win_t.shape[3]
    bpp = page // SEL_BLOCK
    per_b = lambda shape: pl.BlockSpec((None,) + shape, lambda bi, ph: (bi, 0, 0))
    blk_specs = [pl.BlockSpec((None, None, 2 * C_HD, page),
                              functools.partial(lambda bi, ph, j: (l, ph[bi, j] // bpp, 1, 0), j=j)) for j in range(n_blk)]
    grid_spec = pltpu.PrefetchScalarGridSpec(
        num_scalar_prefetch=1,
        grid=(b,),
        in_specs=[per_b((8, C_HD))] + blk_specs +
                 [pl.BlockSpec((None, None, 2 * C_HD, wlen), lambda bi, ph: (l, bi, 0, 0)),
                  per_b((1, LANES)), per_b((1, LANES)), per_b((8, LANES)), per_b((8, LANES))],
        out_specs=per_b((8, LANES)),
    )
    o = pl.pallas_call(
        functools.partial(_nsa_dec2_kernel, n_blk=n_blk),
        grid_spec=grid_spec,
        out_shape=jax.ShapeDtypeStruct((b, 8, LANES), F32),
        compiler_params=_cparams(("arbitrary",), 32),
        name="nsa_decode_attend",
    )(phys, q8, *([pool_t] * n_blk), win_t, selnew, winnew, ocmp, gm)
    return o[:, 0:C_HEADS, C_HD:].reshape(b, C_HEADS * C_HD)


def _nsa_decode(cq, nsa_new, win_new, dtg, pool_t, win_t, l, page_table, cmp_p, g64):
    b = cq.shape[0]
    assert pool_t.shape[3] == 2 * CMP_BLOCK and CMP_BLOCK == SEL_BLOCK
    kcvp = _compress(pool_t, l, cmp_p, g64)
    q4 = cq.astype(F32).reshape(b, C_HEADS, C_HD)
    q_first = jnp.pad(q4, ((0, 0), (0, 8 - C_HEADS), (0, 2 * LANES - C_HD)))
    q_second = jnp.pad(q4, ((0, 0), (0, 8 - C_HEADS), (LANES, LANES - C_HD)))
    ocmp, phys = _nsa_dec1(jnp.concatenate([q_first, q_second], axis=1), page_table, kcvp)
    q8 = jnp.pad(q4, ((0, 0), (0, 8 - C_HEADS), (0, 0)))
    gates = dtg[:, N_DT:N_DT + N_GATE].reshape(b, 3, C_HEADS).transpose(0, 2, 1)
    gm = jnp.pad(gates, ((0, 0), (0, 8 - C_HEADS), (0, LANES - 3)))
    return _nsa_dec2(phys, q8, pool_t, win_t, l, nsa_new[:, None, LANES:], win_new[:, None, :], ocmp, gm)


def kernel(x_prompt, x_sample, cache_diff_kv, cache_nsa_kv, state_nsa_win, state_ssm, state_conv, page_table, ffn1_norm, ffn1_w_gate, ffn1_w_up, ffn1_w_down, mix_norm, w_in, diff_q_norm, diff_k_norm, diff_lambda, diff_subln, conv_w, conv_b, dt_bias, a_log, d_skip, ssm_norm, nsa_q_norm, nsa_k_norm, nsa_cmp_pe, nsa_cmp_w, w_out, ffn2_norm, ffn2_w_gate, ffn2_w_up, ffn2_w_down):
    P = dict(ffn1_norm=ffn1_norm, ffn1_w_gate=ffn1_w_gate, ffn1_w_up=ffn1_w_up, ffn1_w_down=ffn1_w_down, mix_norm=mix_norm,
             w_in=w_in, diff_q_norm=diff_q_norm, diff_k_norm=diff_k_norm, conv_w=conv_w, conv_b=conv_b, dt_bias=dt_bias,
             a_log=a_log, d_skip=d_skip, ssm_norm=ssm_norm, nsa_q_norm=nsa_q_norm, nsa_k_norm=nsa_k_norm,
             nsa_cmp_pe=nsa_cmp_pe, nsa_cmp_w=nsa_cmp_w, w_out=w_out, ffn2_norm=ffn2_norm, ffn2_w_gate=ffn2_w_gate,
             ffn2_w_up=ffn2_w_up, ffn2_w_down=ffn2_w_down)
    depth = w_in.shape[0]
    xp = x_prompt[0]
    xs = x_sample[:, 0]
    s, b = xp.shape[0], xs.shape[0]
    n_pool, page = cache_diff_kv.shape[1], cache_diff_kv.shape[2]
    tm = min(ROW_TILE, s)
    wlen = min(WINDOW, s)
    diff_t = cache_diff_kv.transpose(0, 1, 3, 4, 5, 2).reshape(depth, n_pool, 2 * A_HEADS * A_HD, page)
    nsa_t = cache_nsa_kv.transpose(0, 1, 3, 4, 2).reshape(depth, n_pool, 4 * C_HD, page)
    win_t = state_nsa_win.transpose(0, 1, 3, 4, 2).reshape(depth, b, 2 * C_HD, state_nsa_win.shape[2])
    outs = [[] for _ in range(10)]
    for l in range(depth):
        lam_init = 0.8 - 0.6 * math.exp(-0.3 * l)
        p = _prep_layer(l, P)
        sp = _ssd_params(P, l)
        cmp_p = _cmp_params(P, l)
        dl = diff_lambda[l]
        subg = diff_subln[l]
        xp = _ffn(xp, *p["ffn1"], tm=tm)
        xs = _ffn(xs, *p["ffn1"], tm=b)
        qd, dkv, dkvb, z, xbc, dtg, cq, nsa, nsab, win, winb = _inproj(xp, p, tm=tm)
        od = _diff_prompt(qd, dkvb, dl, subg[None, :], lam_init, tq=ATTN_TQ, tk=ATTN_TK)
        osm, ssm_f = _ssd_prompt(z, xbc, dtg, sp)
        on = _nsa_prompt(cq, nsa, nsab, winb, dtg, cmp_p, p["g64"])
        xp = _mixout(xp, od, osm, on, p["w_out"], tm=tm)
        prompt_outs = (dkv.reshape(1, s, 2, A_HEADS, A_HD), nsa.reshape(1, s, 4, C_HD),
                       win[s - wlen:].reshape(1, wlen, 2, C_HD), ssm_f[None], xbc[s - (CONV_W - 1):][None])
        sqd, sdkv, _, sz, sxbc, sdtg, scq, snsa, _, swin, _ = _inproj(xs, p, tm=b)
        sod = _diff_decode(sqd, sdkv, diff_t, l, page_table, dl, jnp.tile(subg, A_HEADS)[None, :], lam_init,
                           n_pg=DECODE_PAGES_PER_STEP)
        sos, sst, scs = _ssd_decode(sz, sxbc, sdtg, state_conv[l], state_ssm[l], sp)
        son = _nsa_decode(scq, snsa, swin, sdtg, nsa_t, win_t, l, page_table, cmp_p, p["g64"])
        new_win = jnp.concatenate([state_nsa_win[l][:, 1:], swin.reshape(b, 1, 2, C_HD)], axis=1)
        xs = _mixout(xs, sod, sos, son, p["w_out"], tm=b)
        sample_outs = (sdkv.reshape(b, 1, 2, A_HEADS, A_HD), snsa.reshape(b, 1, 4, C_HD), new_win, sst, scs)
        for acc, o in zip(outs, prompt_outs + sample_outs):
            acc.append(o)
        xp = _ffn(xp, *p["ffn2"], tm=tm)
        xs = _ffn(xs, *p["ffn2"], tm=b)
    return (xp[None], xs[:, None]) + tuple(jnp.stack(o) for o in outs)
```

```python
import functools
import math

import jax
import jax.numpy as jnp
import numpy as np
from jax import lax
from jax.experimental import pallas as pl
from jax.experimental.pallas import tpu as pltpu

F32, BF16, I32 = jnp.float32, jnp.bfloat16, jnp.int32

A_HEADS, A_QK, A_HD = 4, 32, 64
M_HEADS, M_HD, M_INNER, M_GROUPS, M_STATE, CONV_W, M_CONV_DIM, M_CHUNK = 8, 64, 512, 2, 128, 4, 1024, 128
C_HEADS, C_HD, CMP_BLOCK, SEL_BLOCK, N_SEL, WINDOW = 4, 64, 64, 64, 16, 512
EPS = 1e-6
NEG = -1e30
M_INIT = -1e29
ONES_ROWS = 8
LOG2E = math.log2(math.e)
LANES = 128
OFF_AQ, OFF_AK, OFF_AV, OFF_Z, OFF_XBC, OFF_CQ, OFF_CKV, OFF_DTG, W_IN_COLS = 0, 256, 512, 768, 1280, 2304, 2560, 2944, 3072
N_DT = M_HEADS
N_GATE = 3 * C_HEADS
ROW_TILE = 512
ATTN_TQ, ATTN_TK = 512, 1024
DECODE_PAGES_PER_STEP = 32


def _cparams(sem, vmem_mib):
    return pltpu.CompilerParams(dimension_semantics=sem, vmem_limit_bytes=vmem_mib * 1024 * 1024)


def _dot(a, b):
    return jnp.dot(a, b, preferred_element_type=F32)


def _dot_nt(a, b):
    return lax.dot_general(a, b, (((1,), (1,)), ((), ())), preferred_element_type=F32)


def _split3(x):
    x1 = x.astype(BF16)
    r = x - x1.astype(F32)
    x2 = r.astype(BF16)
    x3 = (r - x2.astype(F32)).astype(BF16)
    return x1, x2, x3


def _dot_exact_rhs(x, a_bf):
    x1, x2, x3 = _split3(x)
    return _dot(x1, a_bf) + _dot(x2, a_bf) + _dot(x3, a_bf)


def _dot_exact_lhs(a_bf, x):
    x1, x2, x3 = _split3(x)
    return _dot(a_bf, x1) + _dot(a_bf, x2) + _dot(a_bf, x3)


def _sigmoid(x):
    return 1.0 / (1.0 + jnp.exp(-x))


def _silu(x):
    return x * _sigmoid(x)


def _softplus(x):
    return jnp.maximum(x, 0.0) + jnp.log1p(jnp.exp(-jnp.abs(x)))


def _rms_rows(x, g):
    return x * lax.rsqrt(jnp.mean(x * x, axis=-1, keepdims=True) + EPS) * g


def _group_norm(v, gmat_bf):
    sq = v * v
    hi = sq.astype(BF16)
    lo = (sq - hi.astype(F32)).astype(BF16)
    ms = _dot(hi, gmat_bf) + _dot(lo, gmat_bf)
    return v * lax.rsqrt(ms + EPS)


def _ffn_kernel(x_ref, g_ref, wg_ref, wu_ref, wd_ref, o_ref, h_sc, acc_sc):
    c = pl.program_id(1)

    @pl.when(c == 0)
    def _():
        h_sc[...] = _rms_rows(x_ref[...], g_ref[...]).astype(BF16)
        acc_sc[...] = jnp.zeros_like(acc_sc)

    h = h_sc[...]
    a = _dot(h, wg_ref[...])
    b = _dot(h, wu_ref[...])
    acc_sc[...] += _dot((_silu(a) * b).astype(BF16), wd_ref[...])

    @pl.when(c == pl.num_programs(1) - 1)
    def _():
        o_ref[...] = x_ref[...] + 0.5 * acc_sc[...]


def _ffn(x, g, wg, wu, wd, tm):
    m, d = x.shape
    ff = wg.shape[1]
    fc = ff // 2
    return pl.pallas_call(
        _ffn_kernel,
        grid=(m // tm, ff // fc),
        in_specs=[
            pl.BlockSpec((tm, d), lambda i, c: (i, 0)),
            pl.BlockSpec((1, d), lambda i, c: (0, 0)),
            pl.BlockSpec((d, fc), lambda i, c: (0, c)),
            pl.BlockSpec((d, fc), lambda i, c: (0, c)),
            pl.BlockSpec((fc, d), lambda i, c: (c, 0)),
        ],
        out_specs=pl.BlockSpec((tm, d), lambda i, c: (i, 0)),
        out_shape=jax.ShapeDtypeStruct((m, d), F32),
        scratch_shapes=[pltpu.VMEM((tm, d), BF16), pltpu.VMEM((tm, d), F32)],
        compiler_params=_cparams(("parallel", "arbitrary"), 56),
        name="ffn",
    )(x, g, wg, wu, wd)


def _inproj_kernel(x_ref, g_ref, w_ref, g32_ref, g64_ref, qg_ref, kg_ref, cqg_ref, ng_ref, nm_ref, wg_ref, wm_ref,
                   qd_ref, dkv_ref, dkvb_ref, z_ref, xbc_ref, dtg_ref, cq_ref, nsa_ref, nsab_ref,
                   win_ref, winb_ref):
    h = _rms_rows(x_ref[...], g_ref[...]).astype(BF16)

    def seg(a, b):
        return _dot(h, w_ref[:, a:b])

    g32 = g32_ref[...]
    g64 = g64_ref[...]
    qd_ref[...] = (_group_norm(seg(OFF_AQ, OFF_AK), g32) * qg_ref[...]).astype(BF16)
    kn = _group_norm(seg(OFF_AK, OFF_AV), g32) * kg_ref[...]
    av = seg(OFF_AV, OFF_Z)
    dkv_ref[:, 0:256] = kn
    dkv_ref[:, 256:512] = av
    dkvb_ref[:, 0:256] = kn.astype(BF16)
    dkvb_ref[:, 256:512] = av.astype(BF16)
    z_ref[...] = seg(OFF_Z, OFF_XBC)
    xbc_ref[...] = seg(OFF_XBC, OFF_CQ)
    cq_ref[...] = (_group_norm(seg(OFF_CQ, OFF_CKV), g64) * cqg_ref[...]).astype(BF16)
    nr = seg(OFF_CKV, OFF_CKV + 256)
    nsa = jnp.where(nm_ref[...] > 0.5, _group_norm(nr, g64) * ng_ref[...], nr)
    nsa_ref[...] = nsa
    nsab_ref[...] = nsa.astype(BF16)
    wr = seg(OFF_CKV + 256, OFF_DTG)
    win = jnp.where(wm_ref[...] > 0.5, _group_norm(wr, g64_ref[0:128, 0:128]) * wg_ref[...], wr)
    win_ref[...] = win
    winb_ref[...] = win.astype(BF16)
    d = seg(OFF_DTG, W_IN_COLS)
    lane = lax.broadcasted_iota(I32, d.shape, 1)
    dtg_ref[...] = jnp.where(lane >= N_DT, _sigmoid(d), d)


def _inproj(x, p, tm):
    m, d = x.shape
    row = lambda w: pl.BlockSpec((tm, w), lambda i: (i, 0))
    full = lambda a: pl.BlockSpec(a.shape, lambda i: (0,) * a.ndim)
    consts = [p["mix_norm"], p["w_in"], p["g32"], p["g64"], p["qg"], p["kg"], p["cqg"], p["ng"], p["nm"], p["wg"], p["wm"]]
    outs = [(256, BF16), (512, F32), (512, BF16), (512, F32), (1024, F32), (128, F32), (256, BF16),
            (256, F32), (256, BF16), (128, F32), (128, BF16)]
    return pl.pallas_call(
        _inproj_kernel,
        grid=(m // tm,),
        in_specs=[row(d)] + [full(a) for a in consts],
        out_specs=[row(w) for w, _ in outs],
        out_shape=[jax.ShapeDtypeStruct((m, w), dt) for w, dt in outs],
        compiler_params=_cparams(("parallel",), 56),
        name="inproj",
    )(x, *consts)


def _mixout_kernel(x_ref, od_ref, os_ref, on_ref, w_ref, o_ref):
    o_ref[...] = (x_ref[...]
                  + _dot(od_ref[...].astype(BF16), w_ref[0:256, :])
                  + _dot(os_ref[...].astype(BF16), w_ref[256:768, :])
                  + _dot(on_ref[...].astype(BF16), w_ref[768:1024, :]))


def _mixout(x, od, osm, on, w, tm):
    m, d = x.shape
    row = lambda w_: pl.BlockSpec((tm, w_), lambda i: (i, 0))
    return pl.pallas_call(
        _mixout_kernel,
        grid=(m // tm,),
        in_specs=[row(d), row(256), row(512), row(256), pl.BlockSpec(w.shape, lambda i: (0, 0))],
        out_specs=row(d),
        out_shape=jax.ShapeDtypeStruct((m, d), F32),
        compiler_params=_cparams(("parallel",), 40),
        name="mixout",
    )(x, od, osm, on, w)


IN_SPLITS = (256, 256, 256, 512, 1024, 8, 256, 384, 12)


def _block_diag_mean(width, group):
    m = np.zeros((width, width), np.float32)
    for s in range(0, width, group):
        m[s:s + group, s:s + group] = 1.0 / group
    return jnp.asarray(m, BF16)


def _prep_layer(l, P):
    d = P["w_in"].shape[1]
    offs = np.cumsum((0,) + IN_SPLITS)
    cols = [P["w_in"][l][:, offs[i]:offs[i + 1]] for i in range(len(IN_SPLITS))]
    aq, ak, av, z, xbc, dt, cq, ckv, cg = cols
    cg = cg.reshape(d, C_HEADS, 3).transpose(0, 2, 1).reshape(d, N_GATE)
    dtg = jnp.concatenate([dt, cg, jnp.zeros((d, LANES - N_DT - N_GATE), F32)], axis=1)
    w_in = jnp.concatenate([aq, ak, av, z, xbc, cq, ckv, dtg], axis=1).astype(BF16)
    ones64 = jnp.ones((C_HD,), F32)
    kn = P["nsa_k_norm"][l]
    p = {
        "mix_norm": P["mix_norm"][l][None, :],
        "w_in": w_in,
        "g32": _block_diag_mean(256, A_QK),
        "g64": _block_diag_mean(256, C_HD),
        "qg": (jnp.tile(P["diff_q_norm"][l], 2 * A_HEADS) * (A_QK ** -0.5 * LOG2E))[None, :],
        "kg": jnp.tile(P["diff_k_norm"][l], 2 * A_HEADS)[None, :],
        "cqg": (jnp.tile(P["nsa_q_norm"][l], C_HEADS) * (C_HD ** -0.5 * LOG2E))[None, :],
        "ng": jnp.concatenate([ones64, ones64, kn[1], ones64])[None, :],
        "nm": jnp.concatenate([0 * ones64, 0 * ones64, ones64, 0 * ones64])[None, :],
        "wg": jnp.concatenate([kn[2], ones64])[None, :],
        "wm": jnp.concatenate([ones64, 0 * ones64])[None, :],
    }
    for f in ("ffn1", "ffn2"):
        p[f] = (P[f + "_norm"][l][None, :], P[f + "_w_gate"][l].astype(BF16), P[f + "_w_up"][l].astype(BF16),
                P[f + "_w_down"][l].astype(BF16))
    p["w_out"] = P["w_out"][l].astype(BF16)
    return p


def _causal_pairs(n_q, tq, tk, window=None):
    qi, kj, lf, wj = [], [], [], []
    for i in range(n_q):
        last = (i * tq + tq - 1) // tk
        first_w = max(0, (i * tq - (window or 0)) // tk)
        for j in range(last + 1):
            qi.append(i), kj.append(j), lf.append(int(j == last)), wj.append(max(j, first_w))
    return tuple(jnp.asarray(np.asarray(a, np.int32)) for a in (qi, kj, lf, wj))


def _diff_lambda(dl, lam_init):
    a = jnp.sum(dl[0:1] * dl[1:2], axis=-1, keepdims=True)
    b = jnp.sum(dl[2:3] * dl[3:4], axis=-1, keepdims=True)
    return jnp.exp(a) - jnp.exp(b) + lam_init


def _diff_prompt_kernel(qi_ref, kj_ref, lf_ref, q_ref, k_ref, vT_ref, dl_ref, sg_ref, o_ref,
                        qm_sc, m_sc, acc_sc, s_sc, p_sc, *, tq, tk, lam_init):
    t = pl.program_id(0)
    i = qi_ref[t]
    j = kj_ref[t]
    n_hc = 2 * A_HEADS

    @pl.when(j == 0)
    def _():
        q = q_ref[...]
        lane = lax.broadcasted_iota(I32, q.shape, 1)
        for hc in range(n_hc):
            qm_sc[hc] = jnp.where((lane >= A_QK * hc) & (lane < A_QK * (hc + 1)), q, jnp.zeros_like(q))
        m_sc[...] = jnp.full(m_sc.shape, M_INIT, F32)
        acc_sc[...] = jnp.zeros_like(acc_sc)

    def sweep(masked):
        k = k_ref[...]
        if masked:
            kpos = j * tk + lax.broadcasted_iota(I32, (tk, tq), 0)
            qpos = i * tq + lax.broadcasted_iota(I32, (tk, tq), 1)
            keep = kpos <= qpos
        ms = [m_sc[hc] for hc in range(n_hc)]
        new_m = []
        for hc in range(n_hc):
            s = _dot_nt(k, qm_sc[hc])
            if masked:
                s = jnp.where(keep, s, NEG)
            s_sc[hc] = s
            new_m.append(jnp.maximum(ms[hc], jnp.max(s, axis=0, keepdims=True)))
        for hc in range(n_hc):
            p_sc[hc] = jnp.exp2(s_sc[hc] - new_m[hc]).astype(BF16)
        for hc in range(n_hc):
            alpha = jnp.exp2(ms[hc] - new_m[hc])
            acc_sc[hc] = alpha * acc_sc[hc] + _dot(vT_ref[hc // 2], p_sc[hc])
            m_sc[hc] = new_m[hc]

    diag = (j * tk + tk - 1) > (i * tq)
    pl.when(diag)(lambda: sweep(True))
    pl.when(jnp.logical_not(diag))(lambda: sweep(False))

    @pl.when(lf_ref[t] == 1)
    def _():
        lam = _diff_lambda(dl_ref[...], lam_init)
        for h in range(A_HEADS):
            a0, a1 = acc_sc[2 * h], acc_sc[2 * h + 1]
            o = a0[0:A_HD] / a0[A_HD:A_HD + 1] - lam * (a1[0:A_HD] / a1[A_HD:A_HD + 1])
            ms = jnp.mean(o * o, axis=0, keepdims=True)
            o_ref[h] = o * lax.rsqrt(ms + EPS) * sg_ref[...] * (1.0 - lam_init)


def _with_ones_rows(vT):
    return jnp.concatenate([vT, jnp.ones(vT.shape[:-2] + (ONES_ROWS, vT.shape[-1]), vT.dtype)], axis=-2)


def _diff_prompt(qd, dkvb, dl, subg, lam_init, tq, tk):
    s = qd.shape[0]
    assert tk % tq == 0 and s % tk == 0
    vT = _with_ones_rows(dkvb[:, 256:].reshape(s, A_HEADS, A_HD).transpose(1, 2, 0))
    qi, kj, lf, _ = _causal_pairs(s // tq, tq, tk)
    grid_spec = pltpu.PrefetchScalarGridSpec(
        num_scalar_prefetch=3,
        grid=(qi.shape[0],),
        in_specs=[
            pl.BlockSpec((tq, 256), lambda t, qi, kj, lf: (qi[t], 0)),
            pl.BlockSpec((tk, 256), lambda t, qi, kj, lf: (kj[t], 0)),
            pl.BlockSpec((A_HEADS, A_HD + ONES_ROWS, tk), lambda t, qi, kj, lf: (0, 0, kj[t])),
            pl.BlockSpec((4, A_QK), lambda t, qi, kj, lf: (0, 0)),
            pl.BlockSpec((A_HD, 1), lambda t, qi, kj, lf: (0, 0)),
        ],
        out_specs=pl.BlockSpec((A_HEADS, A_HD, tq), lambda t, qi, kj, lf: (0, 0, qi[t])),
        scratch_shapes=[pltpu.VMEM((2 * A_HEADS, tq, 256), BF16), pltpu.VMEM((2 * A_HEADS, 1, tq), F32),
                        pltpu.VMEM((2 * A_HEADS, A_HD + ONES_ROWS, tq), F32),
                        pltpu.VMEM((2 * A_HEADS, tk, tq), F32), pltpu.VMEM((2 * A_HEADS, tk, tq), BF16)],
    )
    o = pl.pallas_call(
        functools.partial(_diff_prompt_kernel, tq=tq, tk=tk, lam_init=lam_init),
        grid_spec=grid_spec,
        out_shape=jax.ShapeDtypeStruct((A_HEADS, A_HD, s), F32),
        compiler_params=_cparams(("arbitrary",), 48),
        name="diff_prompt",
    )(qi, kj, lf, qd, dkvb, vT, dl, subg.reshape(A_HD, 1))
    return o.transpose(2, 0, 1).reshape(s, A_HEADS * A_HD)


def _diff_decode_kernel(pt_ref, q_ref, new_ref, *rest, n_pg, lam_init):
    page_refs = rest[:n_pg]
    dl_ref, sg_ref, o_ref, m_sc, l_sc, acc_sc = rest[n_pg:]
    step = pl.program_id(1)
    n_hc = 2 * A_HEADS
    lane = lax.broadcasted_iota(I32, (n_hc, 256), 1)
    row = lax.broadcasted_iota(I32, (n_hc, 256), 0)
    qm = jnp.where((lane >= A_QK * row) & (lane < A_QK * (row + 1)), q_ref[0], 0.0)

    @pl.when(step == 0)
    def _():
        new = new_ref[0]
        m_sc[...] = jnp.sum(qm * new[:, 0:256], axis=-1, keepdims=True)
        l_sc[...] = jnp.ones_like(l_sc)
        acc_sc[...] = jnp.broadcast_to(new[:, 256:512], acc_sc.shape)

    qb = qm.astype(BF16)
    n_kd = 2 * A_HEADS * A_QK
    s = jnp.concatenate([_dot(qb, r[0:n_kd, :].astype(BF16)) for r in page_refs], axis=1)
    m_prev = m_sc[...]
    m_new = jnp.maximum(m_prev, jnp.max(s, axis=-1, keepdims=True))
    alpha = jnp.exp2(m_prev - m_new)
    p = jnp.exp2(s - m_new)
    l_sc[...] = alpha * l_sc[...] + jnp.sum(p, axis=-1, keepdims=True)
    p = p.astype(BF16)
    page = page_refs[0].shape[1]
    pv = _dot_nt(p[:, 0:page], page_refs[0][n_kd:, :].astype(BF16))
    for g in range(1, n_pg):
        pv = pv + _dot_nt(p[:, g * page:(g + 1) * page], page_refs[g][n_kd:, :].astype(BF16))
    acc_sc[...] = alpha * acc_sc[...] + pv
    m_sc[...] = m_new

    @pl.when(step == pl.num_programs(1) - 1)
    def _():
        lam = _diff_lambda(dl_ref[...], lam_init)
        a = acc_sc[...] / l_sc[...]
        lane1 = lax.broadcasted_iota(I32, (1, 256), 1)
        o = jnp.zeros((1, 256), F32)
        for h in range(A_HEADS):
            in_h = (lane1 >= A_HD * h) & (lane1 < A_HD * (h + 1))
            oh = a[2 * h:2 * h + 1, :] - lam * a[2 * h + 1:2 * h + 2, :]
            ms = jnp.sum(jnp.where(in_h, oh * oh, 0.0), axis=-1, keepdims=True) * (1.0 / A_HD)
            o = jnp.where(in_h, oh * lax.rsqrt(ms + EPS), o)
        o_ref[0] = o * sg_ref[...] * (1.0 - lam_init)


def _diff_decode(qd, dkv_new, pool_t, l, page_table, dl, subg4, lam_init, n_pg):
    b, n_pages = page_table.shape
    page = pool_t.shape[3]
    assert n_pages % n_pg == 0
    cmap = lambda bi, si, pt: (0, 0)
    page_specs = [pl.BlockSpec((None, None, 512, page),
                               functools.partial(lambda bi, si, pt, g: (l, pt[bi, si * n_pg + g], 0, 0), g=g))
                  for g in range(n_pg)]
    grid_spec = pltpu.PrefetchScalarGridSpec(
        num_scalar_prefetch=1,
        grid=(b, n_pages // n_pg),
        in_specs=[pl.BlockSpec((1, 1, 256), lambda bi, si, pt: (bi, 0, 0)),
                  pl.BlockSpec((1, 1, 512), lambda bi, si, pt: (bi, 0, 0))] + page_specs +
                 [pl.BlockSpec((4, A_QK), cmap), pl.BlockSpec((1, 256), cmap)],
        out_specs=pl.BlockSpec((1, 1, 256), lambda bi, si, pt: (bi, 0, 0)),
        scratch_shapes=[pltpu.VMEM((2 * A_HEADS, 1), F32), pltpu.VMEM((2 * A_HEADS, 1), F32),
                        pltpu.VMEM((2 * A_HEADS, 256), F32)],
    )
    o = pl.pallas_call(
        functools.partial(_diff_decode_kernel, n_pg=n_pg, lam_init=lam_init),
        grid_spec=grid_spec,
        out_shape=jax.ShapeDtypeStruct((b, 1, 256), F32),
        compiler_params=_cparams(("parallel", "arbitrary"), 48),
        name="diff_decode",
    )(page_table, qd.astype(F32).reshape(b, 1, 256), dkv_new.reshape(b, 1, 512), *([pool_t] * n_pg), dl, subg4)
    return o.reshape(b, 256)


def _ssd_consts():
    tri = np.tril(np.ones((M_CHUNK, M_CHUNK), np.float32))
    expand = np.zeros((LANES, M_INNER), np.float32)
    for h in range(M_HEADS):
        expand[h, h * M_HD:(h + 1) * M_HD] = 1.0
    return jnp.asarray(tri, BF16), jnp.asarray(tri.T, BF16), jnp.asarray(expand, BF16)


def _ssd_params(P, l):
    pad = lambda v: jnp.concatenate([v, jnp.zeros((LANES - M_HEADS,), F32)])[None, :]
    return dict(cw=P["conv_w"][l], cb=P["conv_b"][l][None, :], dtb=pad(P["dt_bias"][l]), dtbT=P["dt_bias"][l][:, None],
                alog=pad(P["a_log"][l]), alogT=P["a_log"][l][:, None],
                dskip=jnp.repeat(P["d_skip"][l], M_HD)[None, :], nrm=P["ssm_norm"][l][None, :])


def _gated_group_norm(y, x, z, dskip, nrm):
    g = (y + x * dskip) * _silu(z)
    gw = M_INNER // M_GROUPS
    return jnp.concatenate([_rms_rows(g[:, i * gw:(i + 1) * gw], nrm[:, i * gw:(i + 1) * gw]) for i in range(M_GROUPS)], axis=1)


def _ssd_prompt_kernel(z_ref, xbc_ref, dtg_ref, dtT_ref, cw_ref, cb_ref, dtb_ref, dtbT_ref, alog_ref, alogT_ref,
                       dskip_ref, nrm_ref, tri_ref, triT_ref, ex_ref, y_ref, hT_ref, xpad_sc, h_sc):
    c = pl.program_id(0)
    q = M_CHUNK

    @pl.when(c == 0)
    def _():
        xpad_sc[0:8, :] = jnp.zeros((8, M_CONV_DIM), F32)
        h_sc[...] = jnp.zeros_like(h_sc)

    xpad_sc[8:8 + q, :] = xbc_ref[...]
    conv = cb_ref[...]
    for j in range(CONV_W):
        conv = conv + cw_ref[j:j + 1, :] * xpad_sc[8 - (CONV_W - 1) + j:8 - (CONV_W - 1) + j + q, :]
    xpad_sc[0:8, :] = xbc_ref[q - 8:q, :]
    xa = _silu(conv)
    x = xa[:, 0:M_INNER]
    bm = xa[:, M_INNER:M_INNER + M_GROUPS * M_STATE]
    cm = xa[:, M_INNER + M_GROUPS * M_STATE:]

    lane = lax.broadcasted_iota(I32, (1, LANES), 1)
    is_head = lane < M_HEADS
    a_row = jnp.where(is_head, -jnp.exp(alog_ref[...]), 0.0)
    dt = jnp.where(is_head, _softplus(dtg_ref[...] + dtb_ref[...]), 0.0)
    acum = _dot_exact_lhs(tri_ref[...], dt * a_row)
    daT = _softplus(dtT_ref[...] + dtbT_ref[...]) * (-jnp.exp(alogT_ref[...]))
    acumT = _dot_exact_rhs(daT, triT_ref[...])
    ex = ex_ref[...]
    acum_x = _dot_exact_rhs(acum, ex)
    dt_x = _dot_exact_rhs(dt, ex)
    last = acum_x[q - 1:q, :]
    xdt = x * dt_x
    xdd = (xdt * jnp.exp(last - acum_x)).astype(BF16)
    h_prev = h_sc[...]

    rowi = lax.broadcasted_iota(I32, (q, q), 0)
    coli = lax.broadcasted_iota(I32, (q, q), 1)
    causal = rowi >= coli
    lane_q = lax.broadcasted_iota(I32, (q, LANES), 1)
    hpg = M_HEADS // M_GROUPS
    gw = hpg * M_HD
    y_diag, y_off, st = [], [], []
    for g in range(M_GROUPS):
        c_g = cm[:, g * M_STATE:(g + 1) * M_STATE].astype(BF16)
        bT_g = bm[:, g * M_STATE:(g + 1) * M_STATE].T.astype(BF16)
        y_off.append(_dot(c_g, h_prev[:, g * gw:(g + 1) * gw].astype(BF16)))
        st.append(_dot(bT_g, xdd[:, g * gw:(g + 1) * gw]))
        cb = _dot(c_g, bT_g)
        for pr in range(hpg // 2):
            ms = []
            for hh in range(2):
                h = g * hpg + pr * 2 + hh
                seg = acum[:, h:h + 1] - acumT[h:h + 1, :]
                ms.append(cb * jnp.where(causal, jnp.exp(jnp.where(causal, seg, 0.0)), 0.0))
            xp = xdt[:, (g * hpg + pr * 2) * M_HD:(g * hpg + pr * 2 + 2) * M_HD]
            rhs = jnp.concatenate([jnp.where(lane_q < M_HD, xp, 0.0), jnp.where(lane_q >= M_HD, xp, 0.0)], axis=0)
            y_diag.append(_dot(jnp.concatenate(ms, axis=1).astype(BF16), rhs.astype(BF16)))
    y = jnp.concatenate(y_diag, axis=1) + jnp.concatenate(y_off, axis=1) * jnp.exp(acum_x)
    h_new = h_prev * jnp.exp(last) + jnp.concatenate(st, axis=1)
    h_sc[...] = h_new
    y_ref[...] = _gated_group_norm(y, x, z_ref[...], dskip_ref[...], nrm_ref[...])

    @pl.when(c == pl.num_programs(0) - 1)
    def _():
        hT_ref[...] = h_new


def _ssd_prompt(z, xbc, dtg, sp):
    s = z.shape[0]
    q = M_CHUNK
    tri, triT, ex = _ssd_consts()
    dtT = dtg[:, :M_HEADS].T
    consts = [sp["cw"], sp["cb"], sp["dtb"], sp["dtbT"], sp["alog"], sp["alogT"], sp["dskip"], sp["nrm"], tri, triT, ex]
    row = lambda w: pl.BlockSpec((q, w), lambda c: (c, 0))
    y, hT = pl.pallas_call(
        _ssd_prompt_kernel,
        grid=(s // q,),
        in_specs=[row(M_INNER), row(M_CONV_DIM), row(LANES), pl.BlockSpec((M_HEADS, q), lambda c: (0, c))]
                 + [pl.BlockSpec(a.shape, lambda c: (0, 0)) for a in consts],
        out_specs=[row(M_INNER), pl.BlockSpec((M_STATE, M_INNER), lambda c: (0, 0))],
        out_shape=[jax.ShapeDtypeStruct((s, M_INNER), F32), jax.ShapeDtypeStruct((M_STATE, M_INNER), F32)],
        scratch_shapes=[pltpu.VMEM((8 + q, M_CONV_DIM), F32), pltpu.VMEM((M_STATE, M_INNER), F32)],
        compiler_params=_cparams(("arbitrary",), 40),
        name="ssd_prompt",
    )(z, xbc, dtg, dtT, *consts)
    return y, hT.reshape(M_STATE, M_HEADS, M_HD).transpose(1, 2, 0)


def _ssd_decode_kernel(z_ref, xbc_ref, dtg_ref, cs_ref, st_ref, cw_ref, cb_ref, dtb_ref, alog_ref, dskip_ref, nrm_ref,
                       ex_ref, y_ref, sto_ref, cso_ref):
    b = z_ref.shape[0]
    xbc = xbc_ref[...]
    conv = cb_ref[...] + cw_ref[CONV_W - 1:CONV_W, :] * xbc
    for j in range(CONV_W - 1):
        conv = conv + cw_ref[j:j + 1, :] * cs_ref[j]
        if j > 0:
            cso_ref[j - 1] = cs_ref[j]
    cso_ref[CONV_W - 2] = xbc
    xa = _silu(conv)
    x = xa[:, 0:M_INNER]
    bm = xa[:, M_INNER:M_INNER + M_GROUPS * M_STATE]
    cm = xa[:, M_INNER + M_GROUPS * M_STATE:]
    lane = lax.broadcasted_iota(I32, (1, LANES), 1)
    is_head = lane < M_HEADS
    dt = jnp.where(is_head, _softplus(dtg_ref[...] + dtb_ref[...]), 0.0)
    da = dt * jnp.where(is_head, -jnp.exp(alog_ref[...]), 0.0)
    ex = ex_ref[...]
    xdt = x * _dot_exact_rhs(dt, ex)
    dec = jnp.exp(_dot_exact_rhs(da, ex))
    stack = jnp.concatenate([xdt, dec, jnp.zeros((LANES - 2 * b, M_INNER), F32)], axis=0)
    cols = stack.T
    lane_c = lax.broadcasted_iota(I32, (M_INNER, LANES), 1)
    ycols = jnp.zeros((M_INNER, LANES), F32)
    gw = M_INNER // M_GROUPS
    for s in range(b):
        xcol = cols[:, s:s + 1]
        dcol = cols[:, b + s:b + s + 1]
        h = st_ref[s]
        parts = []
        for g in range(M_GROUPS):
            rows = slice(g * gw, (g + 1) * gw)
            hn = h[rows] * dcol[rows] + xcol[rows] * bm[s:s + 1, g * M_STATE:(g + 1) * M_STATE]
            sto_ref[s, rows, :] = hn
            parts.append(jnp.sum(hn * cm[s:s + 1, g * M_STATE:(g + 1) * M_STATE], axis=-1, keepdims=True))
        ycols = jnp.where(lane_c == s, jnp.concatenate(parts, axis=0), ycols)
    y = ycols.T[0:b, :]
    y_ref[...] = _gated_group_norm(y, x, z_ref[...], dskip_ref[...], nrm_ref[...])


def _ssd_decode(z, xbc, dtg, conv_state, ssm_state, sp):
    b = z.shape[0]
    assert 2 * b <= LANES and b % 8 == 0
    _, _, ex = _ssd_consts()
    y, st, cs = pl.pallas_call(
        _ssd_decode_kernel,
        out_shape=[jax.ShapeDtypeStruct((b, M_INNER), F32), jax.ShapeDtypeStruct((b, M_INNER, M_STATE), F32),
                   jax.ShapeDtypeStruct((CONV_W - 1, b, M_CONV_DIM), F32)],
        compiler_params=pltpu.CompilerParams(vmem_limit_bytes=48 * 1024 * 1024),
        name="ssd_decode",
    )(z, xbc, dtg, conv_state.transpose(1, 0, 2), ssm_state.reshape(b, M_INNER, M_STATE),
      sp["cw"], sp["cb"], sp["dtb"], sp["alog"], sp["dskip"], sp["nrm"], ex)
    return y, st.reshape(b, M_HEADS, M_HD, M_STATE), cs.transpose(1, 0, 2)


CMP_ROWS = 2 * C_HD
ROWS_PER_STEP = 16
PAGE_TILE = 512


def _cmp_params(P, l):
    w = P["nsa_cmp_w"][l].transpose(0, 2, 1, 3)
    zeros = jnp.zeros_like(w[0])
    lanes_k = jnp.concatenate([w[0], zeros], axis=-1)
    lanes_v = jnp.concatenate([zeros, w[1]], axis=-1)
    per_blk = jnp.concatenate([lanes_k, lanes_v], axis=0)
    z = jnp.zeros_like(per_blk)
    wn = jnp.concatenate([jnp.concatenate([per_blk, z], axis=-1),
                          jnp.concatenate([z, per_blk], axis=-1)], axis=1).astype(BF16)
    pe = P["nsa_cmp_pe"][l].transpose(1, 2, 0).reshape(CMP_ROWS, CMP_BLOCK)
    kn = P["nsa_k_norm"][l][0]
    one = jnp.ones((C_HD,), F32)
    return wn, jnp.tile(pe, (1, 2)), jnp.concatenate([kn, one, kn, one])[None, :]


def _compress_kernel(x_ref, pe_ref, w_ref, g64_ref, kn_ref, o_ref, acc_sc):
    r = pl.program_id(1)

    @pl.when(r == 0)
    def _():
        acc_sc[...] = jnp.zeros_like(acc_sc)

    acc = acc_sc[...]
    tp = x_ref.shape[0]
    rows = x_ref.reshape(tp * ROWS_PER_STEP, LANES)
    for dd in range(ROWS_PER_STEP):
        x = rows[pl.ds(dd, tp, stride=ROWS_PER_STEP), :]
        acc = acc + _dot((x + pe_ref[dd:dd + 1, :]).astype(BF16), w_ref[dd])
    acc_sc[...] = acc

    @pl.when(r == pl.num_programs(1) - 1)
    def _():
        lane = lax.broadcasted_iota(I32, acc.shape, 1)
        o_ref[...] = jnp.where((lane & C_HD) == 0, _group_norm(acc, g64_ref[...]) * kn_ref[...], acc)


def _compress(x_t, l, cmp_p, g64):
    n = x_t.shape[1]
    tp = n if n <= PAGE_TILE else PAGE_TILE
    assert n % tp == 0
    wn, pe, kn = cmp_p
    return pl.pallas_call(
        _compress_kernel,
        grid=(n // tp, CMP_ROWS // ROWS_PER_STEP),
        in_specs=[pl.BlockSpec((None, tp, ROWS_PER_STEP, LANES), lambda i, r: (l, i, r, 0)),
                  pl.BlockSpec((ROWS_PER_STEP, LANES), lambda i, r: (r, 0)),
                  pl.BlockSpec((ROWS_PER_STEP, LANES, 2 * LANES), lambda i, r: (r, 0, 0)),
                  pl.BlockSpec(g64.shape, lambda i, r: (0, 0)),
                  pl.BlockSpec(kn.shape, lambda i, r: (0, 0))],
        out_specs=pl.BlockSpec((tp, 2 * LANES), lambda i, r: (i, 0)),
        out_shape=jax.ShapeDtypeStruct((n, 2 * LANES), F32),
        scratch_shapes=[pltpu.VMEM((tp, 2 * LANES), F32)],
        compiler_params=_cparams(("parallel", "arbitrary"), 40),
        name="nsa_compress",
    )(x_t, pe, wn, g64, kn)


def _pick_step(ids, axis, on_pick):
    def body(it, carry):
        vals, state = carry
        m = jnp.max(vals, axis=axis, keepdims=True)
        first = jnp.min(jnp.where(vals == m, ids, 1e9), axis=axis, keepdims=True)
        pick = (ids == first) & (m >= 0.0)
        return jnp.where(pick, -1.0, vals), on_pick(it, pick, state)

    return body


def _nsa_select_kernel(q_ref, kc_ref, vcT_ref, ocmp_ref, sel_ref, *, tq):
    i = pl.program_id(0)
    kc = kc_ref[...]
    vcT = vcT_ref[...]
    nb = kc.shape[0]
    blk = lax.broadcasted_iota(I32, (nb, tq), 0)
    pos = i * tq + lax.broadcasted_iota(I32, (nb, tq), 1)
    cmask = (blk + 1) * CMP_BLOCK - 1 <= pos
    imp = jnp.zeros((nb, tq), F32)
    for h in range(C_HEADS):
        s = jnp.where(cmask, _dot_nt(kc, q_ref[h]), NEG)
        e = jnp.where(cmask, jnp.exp2(s - jnp.max(s, axis=0, keepdims=True)), 0.0)
        den = jnp.sum(e, axis=0, keepdims=True)
        p = e / jnp.where(den > 0.0, den, 1.0)
        ocmp_ref[h] = _dot(vcT, p.astype(BF16))
        imp = imp + p
    cur = lax.shift_right_logical(pos, int(math.log2(SEL_BLOCK)))
    vals = jnp.where(blk < cur, imp, -1.0)
    sel = jnp.where(blk == cur, 1.0, 0.0)
    body = _pick_step(blk.astype(F32), 0, lambda it, pick, sel: jnp.where(pick, 1.0, sel))
    _, sel = lax.fori_loop(0, N_SEL - 1, body, (vals, sel))
    sel_ref[...] = sel.astype(BF16)


def _nsa_select(q, kc, vcT, tq):
    s = q.shape[1]
    nb = kc.shape[0]
    return pl.pallas_call(
        functools.partial(_nsa_select_kernel, tq=tq),
        grid=(s // tq,),
        in_specs=[pl.BlockSpec((C_HEADS, tq, C_HD), lambda i: (0, i, 0)), pl.BlockSpec((nb, C_HD), lambda i: (0, 0)),
                  pl.BlockSpec((C_HD, nb), lambda i: (0, 0))],
        out_specs=[pl.BlockSpec((C_HEADS, C_HD, tq), lambda i: (0, 0, i)), pl.BlockSpec((nb, tq), lambda i: (0, i))],
        out_shape=[jax.ShapeDtypeStruct((C_HEADS, C_HD, s), F32), jax.ShapeDtypeStruct((nb, s), BF16)],
        compiler_params=_cparams(("parallel",), 40),
        name="nsa_select",
    )(q, kc, vcT)


def _nsa_attn_kernel(qi_ref, kj_ref, lf_ref, wj_ref, q_ref, ks_ref, vsT_ref, kw_ref, vwT_ref, sel_ref, ocmp_ref, g_ref,
                     o_ref, m_sc, acc_sc, s_sc, p_sc, *, tq, tk):
    t = pl.program_id(0)
    i = qi_ref[t]
    j = kj_ref[t]
    nh = C_HEADS

    @pl.when(j == 0)
    def _():
        m_sc[...] = jnp.full(m_sc.shape, M_INIT, F32)
        acc_sc[...] = jnp.zeros_like(acc_sc)

    kpos = j * tk + lax.broadcasted_iota(I32, (tk, tq), 0)
    qpos = i * tq + lax.broadcasted_iota(I32, (tk, tq), 1)
    nb = sel_ref.shape[0]
    key_blk = lax.shift_right_logical(j * tk + lax.broadcasted_iota(I32, (tk, nb), 0), int(math.log2(SEL_BLOCK)))
    expand = jnp.where(lax.broadcasted_iota(I32, (tk, nb), 1) == key_blk, 1.0, 0.0).astype(BF16)
    picked = _dot(expand, sel_ref[...])
    valid = (picked > 0.5) & (kpos <= qpos)

    def branch(k, vT, ok, base):
        ms = [m_sc[base + h] for h in range(nh)]
        new_m = []
        for h in range(nh):
            s = jnp.where(ok, _dot_nt(k, q_ref[h]), NEG)
            s_sc[h] = s
            new_m.append(jnp.maximum(ms[h], jnp.max(s, axis=0, keepdims=True)))
        for h in range(nh):
            p_sc[h] = jnp.exp2(s_sc[h] - new_m[h]).astype(BF16)
        for h in range(nh):
            alpha = jnp.exp2(ms[h] - new_m[h])
            acc_sc[base + h] = alpha * acc_sc[base + h] + _dot(vT, p_sc[h])
            m_sc[base + h] = new_m[h]

    branch(ks_ref[...], vsT_ref[...], valid, 0)

    @pl.when(wj_ref[t] == j)
    def _():
        branch(kw_ref[...], vwT_ref[...], (kpos <= qpos) & (kpos >= qpos - WINDOW), nh)

    @pl.when(lf_ref[t] == 1)
    def _():
        g = g_ref[...]
        for h in range(nh):
            gate = lambda br: g[br * nh + h:br * nh + h + 1, :]
            a_s, a_w = acc_sc[h], acc_sc[nh + h]
            o_ref[h] = (gate(0) * ocmp_ref[h] + gate(1) * (a_s[0:C_HD] / a_s[C_HD:C_HD + 1])
                        + gate(2) * (a_w[0:C_HD] / a_w[C_HD:C_HD + 1]))


def _nsa_attn(q, ks, vsT, kw, vwT, selT, ocmpT, gT, tq, tk):
    s = q.shape[1]
    nb = selT.shape[0]
    assert tk % tq == 0 and s % tk == 0 and (WINDOW % tk == 0 or tk % WINDOW == 0)
    qi, kj, lf, wj = _causal_pairs(s // tq, tq, tk, WINDOW)
    tq_lanes = lambda rows: pl.BlockSpec((rows, tq), lambda t, qi, kj, lf, wj: (0, qi[t]))
    hd_q = pl.BlockSpec((C_HEADS, C_HD, tq), lambda t, qi, kj, lf, wj: (0, 0, qi[t]))
    grid_spec = pltpu.PrefetchScalarGridSpec(
        num_scalar_prefetch=4,
        grid=(qi.shape[0],),
        in_specs=[
            pl.BlockSpec((C_HEADS, tq, C_HD), lambda t, qi, kj, lf, wj: (0, qi[t], 0)),
            pl.BlockSpec((tk, C_HD), lambda t, qi, kj, lf, wj: (kj[t], 0)),
            pl.BlockSpec((C_HD + ONES_ROWS, tk), lambda t, qi, kj, lf, wj: (0, kj[t])),
            pl.BlockSpec((tk, C_HD), lambda t, qi, kj, lf, wj: (wj[t], 0)),
            pl.BlockSpec((C_HD + ONES_ROWS, tk), lambda t, qi, kj, lf, wj: (0, wj[t])),
            tq_lanes(nb), hd_q, tq_lanes(N_GATE),
        ],
        out_specs=hd_q,
        scratch_shapes=[pltpu.VMEM((2 * C_HEADS, 1, tq), F32), pltpu.VMEM((2 * C_HEADS, C_HD + ONES_ROWS, tq), F32),
                        pltpu.VMEM((C_HEADS, tk, tq), F32), pltpu.VMEM((C_HEADS, tk, tq), BF16)],
    )
    return pl.pallas_call(
        functools.partial(_nsa_attn_kernel, tq=tq, tk=tk),
        grid_spec=grid_spec,
        out_shape=jax.ShapeDtypeStruct((C_HEADS, C_HD, s), F32),
        compiler_params=_cparams(("arbitrary",), 48),
        name="nsa_attn",
    )(qi, kj, lf, wj, q, ks, _with_ones_rows(vsT), kw, _with_ones_rows(vwT), selT, ocmpT, gT)


def _nsa_prompt(cq, nsa, nsab, winb, dtg, cmp_p, g64):
    s = cq.shape[0]
    q = cq.reshape(s, C_HEADS, C_HD).transpose(1, 0, 2)
    x_t = nsa[:, 0:CMP_ROWS].reshape(s // LANES, LANES, CMP_ROWS).transpose(0, 2, 1)[None]
    kcv = _compress(x_t, 0, cmp_p, g64).reshape(s // CMP_BLOCK, 2 * C_HD)
    ocmpT, selT = _nsa_select(q, kcv[:, :C_HD].astype(BF16), kcv[:, C_HD:].T.astype(BF16), tq=128)
    oT = _nsa_attn(q, nsab[:, 128:192], nsab[:, 192:256].T, winb[:, 0:C_HD], winb[:, C_HD:].T, selT, ocmpT,
                   dtg[:, N_DT:N_DT + N_GATE].T, tq=ATTN_TQ, tk=ATTN_TK)
    return oT.transpose(2, 0, 1).reshape(s, C_HEADS * C_HD)


SEQ_PER_STEP = 8
FLAG_LANE = 16


def _nsa_dec1_kernel(pt_ref, q_ref, ptv_ref, kcvp_ref, ocmp_ref, phys_ref, buf, sem, imp_sc, *, n_pages):
    g = pl.program_id(0)

    def row_copy(sb, pg):
        page = pt_ref[g * SEQ_PER_STEP + sb, pg]
        return pltpu.make_async_copy(kcvp_ref.at[pl.ds(page, 1), :], buf.at[sb, pl.ds(pg, 1), :], sem.at[0])

    def start_all(pg, carry):
        for sb in range(SEQ_PER_STEP):
            row_copy(sb, pg).start()
        return carry

    def wait_all(pg, carry):
        for sb in range(SEQ_PER_STEP):
            row_copy(sb, pg).wait()
        return carry

    lax.fori_loop(0, n_pages, start_all, 0)
    lax.fori_loop(0, n_pages, wait_all, 0)

    row8 = lax.broadcasted_iota(I32, (8, 2 * n_pages), 0)
    for sb in range(SEQ_PER_STEP):
        kcv = buf[sb].astype(BF16)
        s01 = _dot_nt(q_ref[sb].astype(BF16), kcv)
        s = jnp.concatenate([s01[0:8], s01[8:16]], axis=1)
        e = jnp.exp2(s - jnp.max(s, axis=-1, keepdims=True))
        p = e / jnp.sum(e, axis=-1, keepdims=True)
        r0 = _dot(p[:, 0:n_pages].astype(BF16), kcv)
        r1 = _dot(p[:, n_pages:].astype(BF16), kcv)
        ocmp_ref[sb] = r0[:, 0:LANES] + r1[:, LANES:]
        imp_sc[sb:sb + 1, :] = jnp.sum(jnp.where(row8 < C_HEADS, p, 0.0), axis=0, keepdims=True)

    col = lax.broadcasted_iota(I32, (SEQ_PER_STEP, 2 * n_pages), 1)
    second = col >= n_pages
    ids = (2 * jnp.where(second, col - n_pages, col) + jnp.where(second, 1, 0)).astype(F32)
    ptv = ptv_ref[...].astype(F32)
    phys_all = jnp.concatenate([2.0 * ptv, 2.0 * ptv + 1.0], axis=1)
    lane = lax.broadcasted_iota(I32, (SEQ_PER_STEP, LANES), 1)

    def on_pick(it, pick, acc):
        ph = jnp.sum(jnp.where(pick, phys_all, 0.0), axis=-1, keepdims=True)
        ok = jnp.sum(jnp.where(pick, 1.0, 0.0), axis=-1, keepdims=True)
        return jnp.where(lane == it, ph, jnp.where(lane == FLAG_LANE + it, ok, acc))

    _, acc = lax.fori_loop(0, N_SEL - 1, _pick_step(ids, -1, on_pick),
                           (imp_sc[...], jnp.zeros((SEQ_PER_STEP, LANES), F32)))
    phys_ref[...] = acc.astype(I32)


def _nsa_dec1(q01, page_table, kcvp):
    b, n_pages = page_table.shape
    assert b % SEQ_PER_STEP == 0
    grid_spec = pltpu.PrefetchScalarGridSpec(
        num_scalar_prefetch=1,
        grid=(b // SEQ_PER_STEP,),
        in_specs=[pl.BlockSpec((SEQ_PER_STEP, 16, 256), lambda g, pt: (g, 0, 0)),
                  pl.BlockSpec((SEQ_PER_STEP, n_pages), lambda g, pt: (g, 0)),
                  pl.BlockSpec(memory_space=pl.ANY)],
        out_specs=[pl.BlockSpec((SEQ_PER_STEP, 8, LANES), lambda g, pt: (g, 0, 0)),
                   pl.BlockSpec((SEQ_PER_STEP, LANES), lambda g, pt: (g, 0))],
        scratch_shapes=[pltpu.VMEM((SEQ_PER_STEP, n_pages, 256), F32), pltpu.SemaphoreType.DMA((1,)),
                        pltpu.VMEM((SEQ_PER_STEP, 2 * n_pages), F32)],
    )
    return pl.pallas_call(
        functools.partial(_nsa_dec1_kernel, n_pages=n_pages),
        grid_spec=grid_spec,
        out_shape=[jax.ShapeDtypeStruct((b, 8, LANES), F32), jax.ShapeDtypeStruct((b, LANES), I32)],
        compiler_params=_cparams(("arbitrary",), 32),
        name="nsa_decode_select",
    )(page_table, q01, page_table, kcvp)


def _nsa_dec2_kernel(ph_ref, q_ref, *rest, n_blk):
    blk_refs = rest[:n_blk]
    ws_ref, sn_ref, wn_ref, oc_ref, gm_ref, o_ref = rest[n_blk:]
    b = pl.program_id(0)
    q = q_ref[...]
    qb = q.astype(BF16)

    def attend(slabs, valid, new_row):
        slabs = [sl.astype(BF16) for sl in slabs]
        s = jnp.concatenate([_dot(qb, sl[0:C_HD, :]) for sl in slabs], axis=1)
        s_new = jnp.sum(q * new_row[:, 0:C_HD], axis=-1, keepdims=True)
        if valid is not None:
            s = jnp.where(valid, s, NEG)
        m = jnp.maximum(jnp.max(s, axis=-1, keepdims=True), s_new)
        p = jnp.exp2(s - m)
        if valid is not None:
            p = jnp.where(valid, p, 0.0)
        pn = jnp.exp2(s_new - m)
        den = jnp.sum(p, axis=-1, keepdims=True) + pn
        p = p.astype(BF16)
        pv = pn * new_row
        off = 0
        for sl in slabs:
            pv = pv + _dot_nt(p[:, off:off + sl.shape[1]], sl)
            off += sl.shape[1]
        return pv / den

    page = blk_refs[0].shape[1]
    shift = int(math.log2(SEL_BLOCK))
    col = lax.broadcasted_iota(I32, (8, n_blk * page), 1)
    col_slab = lax.shift_right_logical(col, int(math.log2(page)))
    col_half = lax.shift_right_logical(col, shift) & (page // SEL_BLOCK - 1)
    vf = jnp.zeros((8, n_blk * page), F32)
    for j in range(n_blk):
        half = ph_ref[b, j] & (page // SEL_BLOCK - 1)
        flag = ph_ref[b, FLAG_LANE + j].astype(F32)
        vf = jnp.where((col_slab == j) & (col_half == half), flag, vf)
    o_sel = attend([r[...] for r in blk_refs], vf > 0.5, sn_ref[...])
    o_win = attend([ws_ref[...]], None, wn_ref[...])
    gm = gm_ref[...]
    o_ref[...] = gm[:, 0:1] * oc_ref[...] + gm[:, 1:2] * o_sel + gm[:, 2:3] * o_win


def _nsa_dec2(phys, q8, pool_t, win_t, l, selnew, winnew, ocmp, gm):
    b = q8.shape[0]
    n_blk = N_SEL - 1
    page, wlen = pool_t.shape[3], win_t.shape[3]
    bpp = page // SEL_BLOCK
    per_b = lambda shape: pl.BlockSpec((None,) + shape, lambda bi, ph: (bi, 0, 0))
    blk_specs = [pl.BlockSpec((None, None, 2 * C_HD, page),
                              functools.partial(lambda bi, ph, j: (l, ph[bi, j] // bpp, 1, 0), j=j)) for j in range(n_blk)]
    grid_spec = pltpu.PrefetchScalarGridSpec(
        num_scalar_prefetch=1,
        grid=(b,),
        in_specs=[per_b((8, C_HD))] + blk_specs +
                 [pl.BlockSpec((None, None, 2 * C_HD, wlen), lambda bi, ph: (l, bi, 0, 0)),
                  per_b((1, LANES)), per_b((1, LANES)), per_b((8, LANES)), per_b((8, LANES))],
        out_specs=per_b((8, LANES)),
    )
    o = pl.pallas_call(
        functools.partial(_nsa_dec2_kernel, n_blk=n_blk),
        grid_spec=grid_spec,
        out_shape=jax.ShapeDtypeStruct((b, 8, LANES), F32),
        compiler_params=_cparams(("arbitrary",), 32),
        name="nsa_decode_attend",
    )(phys, q8, *([pool_t] * n_blk), win_t, selnew, winnew, ocmp, gm)
    return o[:, 0:C_HEADS, C_HD:].reshape(b, C_HEADS * C_HD)


def _nsa_decode(cq, nsa_new, win_new, dtg, pool_t, win_t, l, page_table, cmp_p, g64):
    b = cq.shape[0]
    assert pool_t.shape[3] == 2 * CMP_BLOCK and CMP_BLOCK == SEL_BLOCK
    kcvp = _compress(pool_t, l, cmp_p, g64)
    q4 = cq.astype(F32).reshape(b, C_HEADS, C_HD)
    q_first = jnp.pad(q4, ((0, 0), (0, 8 - C_HEADS), (0, 2 * LANES - C_HD)))
    q_second = jnp.pad(q4, ((0, 0), (0, 8 - C_HEADS), (LANES, LANES - C_HD)))
    ocmp, phys = _nsa_dec1(jnp.concatenate([q_first, q_second], axis=1), page_table, kcvp)
    q8 = jnp.pad(q4, ((0, 0), (0, 8 - C_HEADS), (0, 0)))
    gates = dtg[:, N_DT:N_DT + N_GATE].reshape(b, 3, C_HEADS).transpose(0, 2, 1)
    gm = jnp.pad(gates, ((0, 0), (0, 8 - C_HEADS), (0, LANES - 3)))
    return _nsa_dec2(phys, q8, pool_t, win_t, l, nsa_new[:, None, LANES:], win_new[:, None, :], ocmp, gm)


def kernel(x_prompt, x_sample, cache_diff_kv, cache_nsa_kv, state_nsa_win, state_ssm, state_conv, page_table, ffn1_norm, ffn1_w_gate, ffn1_w_up, ffn1_w_down, mix_norm, w_in, diff_q_norm, diff_k_norm, diff_lambda, diff_subln, conv_w, conv_b, dt_bias, a_log, d_skip, ssm_norm, nsa_q_norm, nsa_k_norm, nsa_cmp_pe, nsa_cmp_w, w_out, ffn2_norm, ffn2_w_gate, ffn2_w_up, ffn2_w_down):
    P = dict(ffn1_norm=ffn1_norm, ffn1_w_gate=ffn1_w_gate, ffn1_w_up=ffn1_w_up, ffn1_w_down=ffn1_w_down, mix_norm=mix_norm,
             w_in=w_in, diff_q_norm=diff_q_norm, diff_k_norm=diff_k_norm, conv_w=conv_w, conv_b=conv_b, dt_bias=dt_bias,
             a_log=a_log, d_skip=d_skip, ssm_norm=ssm_norm, nsa_q_norm=nsa_q_norm, nsa_k_norm=nsa_k_norm,
             nsa_cmp_pe=nsa_cmp_pe, nsa_cmp_w=nsa_cmp_w, w_out=w_out, ffn2_norm=ffn2_norm, ffn2_w_gate=ffn2_w_gate,
             ffn2_w_up=ffn2_w_up, ffn2_w_down=ffn2_w_down)
    depth = w_in.shape[0]
    xp = x_prompt[0]
    xs = x_sample[:, 0]
    s, b = xp.shape[0], xs.shape[0]
    n_pool, page = cache_diff_kv.shape[1], cache_diff_kv.shape[2]
    tm = min(ROW_TILE, s)
    wlen = min(WINDOW, s)
    diff_t = cache_diff_kv.transpose(0, 1, 3, 4, 5, 2).reshape(depth, n_pool, 2 * A_HEADS * A_HD, page)
    nsa_t = cache_nsa_kv.transpose(0, 1, 3, 4, 2).reshape(depth, n_pool, 4 * C_HD, page)
    win_t = state_nsa_win.transpose(0, 1, 3, 4, 2).reshape(depth, b, 2 * C_HD, state_nsa_win.shape[2])
    outs = [[] for _ in range(10)]
    for l in range(depth):
        lam_init = 0.8 - 0.6 * math.exp(-0.3 * l)
        p = _prep_layer(l, P)
        sp = _ssd_params(P, l)
        cmp_p = _cmp_params(P, l)
        dl = diff_lambda[l]
        subg = diff_subln[l]
        xp = _ffn(xp, *p["ffn1"], tm=tm)
        xs = _ffn(xs, *p["ffn1"], tm=b)
        qd, dkv, dkvb, z, xbc, dtg, cq, nsa, nsab, win, winb = _inproj(xp, p, tm=tm)
        od = _diff_prompt(qd, dkvb, dl, subg[None, :], lam_init, tq=ATTN_TQ, tk=ATTN_TK)
        osm, ssm_f = _ssd_prompt(z, xbc, dtg, sp)
        on = _nsa_prompt(cq, nsa, nsab, winb, dtg, cmp_p, p["g64"])
        xp = _mixout(xp, od, osm, on, p["w_out"], tm=tm)
        prompt_outs = (dkv.reshape(1, s, 2, A_HEADS, A_HD), nsa.reshape(1, s, 4, C_HD),
                       win[s - wlen:].reshape(1, wlen, 2, C_HD), ssm_f[None], xbc[s - (CONV_W - 1):][None])
        sqd, sdkv, _, sz, sxbc, sdtg, scq, snsa, _, swin, _ = _inproj(xs, p, tm=b)
        sod = _diff_decode(sqd, sdkv, diff_t, l, page_table, dl, jnp.tile(subg, A_HEADS)[None, :], lam_init,
                           n_pg=DECODE_PAGES_PER_STEP)
        sos, sst, scs = _ssd_decode(sz, sxbc, sdtg, state_conv[l], state_ssm[l], sp)
        son = _nsa_decode(scq, snsa, swin, sdtg, nsa_t, win_t, l, page_table, cmp_p, p["g64"])
        new_win = jnp.concatenate([state_nsa_win[l][:, 1:], swin.reshape(b, 1, 2, C_HD)], axis=1)
        xs = _mixout(xs, sod, sos, son, p["w_out"], tm=b)
        sample_outs = (sdkv.reshape(b, 1, 2, A_HEADS, A_HD), snsa.reshape(b, 1, 4, C_HD), new_win, sst, scs)
        for acc, o in zip(outs, prompt_outs + sample_outs):
            acc.append(o)
        xp = _ffn(xp, *p["ffn2"], tm=tm)
        xs = _ffn(xs, *p["ffn2"], tm=b)
    return (xp[None], xs[:, None]) + tuple(jnp.stack(o) for o in outs)
```
